```python
import math
import jax, jax.numpy as jnp
from jax import lax
import numpy as np

D_MODEL = 2048
BATCH = 4
SEQ = 8192
DEPTH = 2
DEC_BATCH = 16
DEC_SEQ = 2048
PAST_LEN = 128

HEAD_DIM = 128
GRID_W = 64
BLOCK = 128
ROPE_THETA = 10000.0
EPS = 1e-6
A_HEADS = 8
A_KV_HEADS = 2
WINDOW = 128
B_HEADS = 8
B_KV_HEADS = 2
C_HEADS = 8
NB_ROWS_MAX = 8
NB_COLS = 16
RPB_ROWS = 2 * NB_ROWS_MAX - 1
RPB_COLS = 2 * NB_COLS - 1
N_EXPERTS = 16
EXPERT_FF = 4096
CAPACITY_FACTOR = 2
NEG_INF = -1e30

A_Q = A_HEADS * HEAD_DIM
A_KV = A_KV_HEADS * HEAD_DIM
B_Q = B_HEADS * HEAD_DIM
B_KV = B_KV_HEADS * HEAD_DIM
C_W = C_HEADS * HEAD_DIM
IN_WIDTHS = (A_Q, A_KV, A_KV, B_Q, B_KV, B_KV, C_W, C_W, C_W, D_MODEL, D_MODEL, D_MODEL)
IN_TOTAL = A_Q + 2 * A_KV + B_Q + 2 * B_KV + 3 * C_W + 3 * D_MODEL

kernel_name = "hybrid_window_axial_neighbourhood_ec_encoder"


def rms_norm(x, g):
    x32 = x.astype(jnp.float32)
    y = x32 * lax.rsqrt(jnp.mean(x32 * x32, axis=-1, keepdims=True) + EPS)
    return (y * g.astype(jnp.float32)).astype(x.dtype)


def qk_rms(x, g):
    x32 = x.astype(jnp.float32)
    return x32 * lax.rsqrt(jnp.mean(x32 * x32, axis=-1, keepdims=True) + EPS) * g.astype(jnp.float32)


def rope_cos_sin(pos, dim):
    inv = ROPE_THETA ** (-jnp.arange(0, dim, 2, dtype=jnp.float32) / dim)
    ang = pos.astype(jnp.float32)[:, None] * inv[None, :]
    return jnp.cos(ang), jnp.sin(ang)


def apply_rope(x, cos, sin):
    x1, x2 = jnp.split(x, 2, axis=-1)
    c = cos[:, None, :]
    s = sin[:, None, :]
    return jnp.concatenate([x1 * c - x2 * s, x2 * c + x1 * s], axis=-1)


def apply_axial_rope(x, cos_r, sin_r, cos_c, sin_c):
    xr, xc = jnp.split(x, 2, axis=-1)
    return jnp.concatenate([apply_rope(xr, cos_r, sin_r), apply_rope(xc, cos_c, sin_c)], axis=-1)


def window_sink_attention(q, k, v, sink):
    B, S, HQ, D = q.shape
    HKV = k.shape[2]
    G = HQ // HKV
    nb = S // BLOCK
    qb = q.reshape(B, nb, BLOCK, HKV, G, D)
    pad = ((0, 0), (BLOCK, BLOCK), (0, 0), (0, 0))
    kp = jnp.pad(k, pad).reshape(B, nb + 2, BLOCK, HKV, D)
    vp = jnp.pad(v, pad).reshape(B, nb + 2, BLOCK, HKV, D)
    kband = jnp.concatenate([kp[:, :-2], kp[:, 1:-1], kp[:, 2:]], axis=2)
    vband = jnp.concatenate([vp[:, :-2], vp[:, 1:-1], vp[:, 2:]], axis=2)
    s = jnp.einsum("bnqkgd,bnskd->bnkgqs", qb, kband).astype(jnp.float32) / math.sqrt(D)
    blk = jnp.arange(nb)[:, None]
    qpos = blk * BLOCK + jnp.arange(BLOCK)[None, :]
    kpos = (blk - 1) * BLOCK + jnp.arange(3 * BLOCK)[None, :]
    valid = ((jnp.abs(qpos[:, :, None] - kpos[:, None, :]) <= WINDOW)
             & (kpos >= 0)[:, None, :] & (kpos < S)[:, None, :])
    s = jnp.where(valid[None, :, None, None], s, NEG_INF)
    sink_l = sink.astype(jnp.float32).reshape(HKV, G)[None, None, :, :, None, None]
    m = jnp.maximum(jnp.max(s, axis=-1, keepdims=True), sink_l)
    p = jnp.exp(s - m)
    p = p / (jnp.sum(p, axis=-1, keepdims=True) + jnp.exp(sink_l - m))
    o = jnp.einsum("bnkgqs,bnskd->bnqkgd", p.astype(v.dtype), vband)
    return o.reshape(B, S, HQ, D)


def dense_block_attention(q, k, v):
    B, S, HQ, D = q.shape
    HKV = k.shape[2]
    G = HQ // HKV
    nb = S // BLOCK
    qb = q.reshape(B, nb, BLOCK, HKV, G, D).transpose(1, 0, 2, 3, 4, 5)
    scale = 1.0 / math.sqrt(D)

    def one_block(qblk):
        s = jnp.einsum("bqkgd,bskd->bkgqs", qblk, k).astype(jnp.float32) * scale
        p = jax.nn.softmax(s, axis=-1).astype(v.dtype)
        return jnp.einsum("bkgqs,bskd->bqkgd", p, v)

    o = lax.map(one_block, qb)
    return o.transpose(1, 0, 2, 3, 4, 5).reshape(B, S, HQ, D)


def neighbourhood_attention(q, k, v, rpb):
    B, S, H, D = q.shape
    rows = S // GRID_W
    kh = min(NB_ROWS_MAX, rows)
    qg = q.reshape(B, rows, GRID_W, H, D)
    kg = k.reshape(B, rows, GRID_W, H, D)
    vg = v.reshape(B, rows, GRID_W, H, D)
    cols = jnp.arange(GRID_W)
    c_start = jnp.clip(cols - NB_COLS // 2, 0, GRID_W - NB_COLS)
    col_idx = c_start[:, None] + jnp.arange(NB_COLS)[None, :]
    col_off = col_idx - cols[:, None] + (NB_COLS - 1)
    rpb_cols = rpb.astype(jnp.float32)[:, :, col_off]
    scale = 1.0 / math.sqrt(D)

    def one_row(r):
        r_start = jnp.clip(r - kh // 2, 0, rows - kh)
        k_band = lax.dynamic_slice_in_dim(kg, r_start, kh, axis=1)
        v_band = lax.dynamic_slice_in_dim(vg, r_start, kh, axis=1)
        k_win = k_band[:, :, col_idx]
        v_win = v_band[:, :, col_idx]
        q_row = lax.dynamic_index_in_dim(qg, r, axis=1, keepdims=False)
        s = jnp.einsum("bchd,bicjhd->bhcij", q_row, k_win).astype(jnp.float32) * scale
        row_off = r_start + jnp.arange(kh) - r + (NB_ROWS_MAX - 1)
        bias = rpb_cols[:, row_off].transpose(0, 2, 1, 3)
        s = (s + bias[None]).reshape(B, H, GRID_W, kh * NB_COLS)
        p = jax.nn.softmax(s, axis=-1).reshape(B, H, GRID_W, kh, NB_COLS).astype(v.dtype)
        return jnp.einsum("bhcij,bicjhd->bchd", p, v_win)

    o = lax.map(one_row, jnp.arange(rows))
    return o.transpose(1, 0, 2, 3, 4).reshape(B, S, H, D)


def expert_choice_ffn(h, w_router, w_gate, w_up, w_down):
    T, D = h.shape
    cap = CAPACITY_FACTOR * T // N_EXPERTS
    aff = jax.nn.softmax(h.astype(jnp.float32) @ w_router.astype(jnp.float32), axis=-1)
    g, idx = lax.top_k(aff.T, cap)
    xs = h[idx]
    a = jnp.einsum("ecd,edf->ecf", xs, w_gate)
    u = jnp.einsum("ecd,edf->ecf", xs, w_up)
    ye = jnp.einsum("ecf,efd->ecd", jax.nn.silu(a) * u, w_down)
    ye = ye * g[..., None].astype(ye.dtype)
    return jnp.zeros_like(h).at[idx.reshape(-1)].add(ye.reshape(-1, D))


def trunk_layer(x, norm_attn, w_in, qk_norm, sink_a, rpb_c, w_branch_a, w_branch_b, w_branch_c,
                w_out, norm_ffn, w_router, w_gate, w_up, w_down):
    B, S, _ = x.shape
    dt = x.dtype
    h = rms_norm(x, norm_attn)
    proj = jnp.einsum("bsd,de->bse", h, w_in)
    offs = [int(o) for o in np.cumsum(IN_WIDTHS)[:-1]]
    qa, ka, va, qb, kb, vb, qc, kc, vc, ga, gb, gc = jnp.split(proj, offs, axis=-1)
    heads = lambda t: t.reshape(B, S, -1, HEAD_DIM)

    t = jnp.arange(S, dtype=jnp.int32)
    cos_t, sin_t = rope_cos_sin(t, HEAD_DIM)
    cos_r, sin_r = rope_cos_sin(t // GRID_W, HEAD_DIM // 2)
    cos_c, sin_c = rope_cos_sin(t % GRID_W, HEAD_DIM // 2)

    qa = apply_rope(qk_rms(heads(qa), qk_norm[0]), cos_t, sin_t).astype(dt)
    ka = apply_rope(qk_rms(heads(ka), qk_norm[1]), cos_t, sin_t).astype(dt)
    oa = window_sink_attention(qa, ka, heads(va), sink_a)
    qb = apply_axial_rope(qk_rms(heads(qb), qk_norm[2]), cos_r, sin_r, cos_c, sin_c).astype(dt)
    kb = apply_axial_rope(qk_rms(heads(kb), qk_norm[3]), cos_r, sin_r, cos_c, sin_c).astype(dt)
    ob = dense_block_attention(qb, kb, heads(vb))
    qc = qk_rms(heads(qc), qk_norm[4]).astype(dt)
    kc = qk_rms(heads(kc), qk_norm[5]).astype(dt)
    oc = neighbourhood_attention(qc, kc, heads(vc), rpb_c)

    ya = oa.reshape(B, S, A_Q) @ w_branch_a
    yb = ob.reshape(B, S, B_Q) @ w_branch_b
    yc = oc.reshape(B, S, C_W) @ w_branch_c
    merged = jax.nn.sigmoid(ga) * ya + jax.nn.sigmoid(gb) * yb + jax.nn.sigmoid(gc) * yc
    x = x + merged @ w_out

    h2 = rms_norm(x, norm_ffn)
    y = expert_choice_ffn(h2.reshape(B * S, D_MODEL), w_router, w_gate, w_up, w_down)
    return x + y.reshape(B, S, D_MODEL)


def run_trunk(x, norm_attn, w_in, qk_norm, sink_a, rpb_c, w_branch_a, w_branch_b, w_branch_c,
              w_out, norm_ffn, w_router, w_gate, w_up, w_down):
    for l in range(DEPTH):
        x = trunk_layer(x, norm_attn[l], w_in[l], qk_norm[l], sink_a[l], rpb_c[l],
                        w_branch_a[l], w_branch_b[l], w_branch_c[l], w_out[l], norm_ffn[l],
                        w_router[l], w_gate[l], w_up[l], w_down[l])
    return x


def setup_inputs(seed: int = 0) -> dict:
    key = jax.random.key(seed)
    ks = jax.random.split(key, 16)
    nrm = lambda k, shape, scale: jax.random.normal(k, shape, dtype=jnp.float32) * scale
    return {
        "x_prompt": nrm(ks[0], (BATCH, SEQ, D_MODEL), 1.0),
        "x_sample": nrm(ks[1], (DEC_BATCH, DEC_SEQ, D_MODEL), 1.0),
        "norm_attn": 1.0 + nrm(ks[2], (DEPTH, D_MODEL), 0.05),
        "w_in": nrm(ks[3], (DEPTH, D_MODEL, IN_TOTAL), D_MODEL ** -0.5),
        "qk_norm": 1.0 + nrm(ks[4], (DEPTH, 6, HEAD_DIM), 0.05),
        "sink_a": nrm(ks[5], (DEPTH, A_HEADS), 0.5),
        "rpb_c": nrm(ks[6], (DEPTH, C_HEADS, RPB_ROWS, RPB_COLS), 0.1),
        "w_branch_a": nrm(ks[7], (DEPTH, A_Q, D_MODEL), A_Q ** -0.5),
        "w_branch_b": nrm(ks[8], (DEPTH, B_Q, D_MODEL), B_Q ** -0.5),
        "w_branch_c": nrm(ks[9], (DEPTH, C_W, D_MODEL), C_W ** -0.5),
        "w_out": nrm(ks[10], (DEPTH, D_MODEL, D_MODEL), D_MODEL ** -0.5),
        "norm_ffn": 1.0 + nrm(ks[11], (DEPTH, D_MODEL), 0.05),
        "w_router": nrm(ks[12], (DEPTH, D_MODEL, N_EXPERTS), D_MODEL ** -0.5),
        "w_gate": nrm(ks[13], (DEPTH, N_EXPERTS, D_MODEL, EXPERT_FF), D_MODEL ** -0.5),
        "w_up": nrm(ks[14], (DEPTH, N_EXPERTS, D_MODEL, EXPERT_FF), D_MODEL ** -0.5),
        "w_down": nrm(ks[15], (DEPTH, N_EXPERTS, EXPERT_FF, D_MODEL), EXPERT_FF ** -0.5),
    }


def reference(x_prompt, x_sample, norm_attn, w_in, qk_norm, sink_a, rpb_c, w_branch_a, w_branch_b,
              w_branch_c, w_out, norm_ffn, w_router, w_gate, w_up, w_down):
    y_prompt = run_trunk(x_prompt, norm_attn, w_in, qk_norm, sink_a, rpb_c, w_branch_a, w_branch_b,
                         w_branch_c, w_out, norm_ffn, w_router, w_gate, w_up, w_down)
    y_sample = run_trunk(x_sample, norm_attn, w_in, qk_norm, sink_a, rpb_c, w_branch_a, w_branch_b,
                         w_branch_c, w_out, norm_ffn, w_router, w_gate, w_up, w_down)
    return (y_prompt, y_sample)
```

```python
import functools
import math

import jax
import jax.numpy as jnp
from jax import lax
from jax.experimental import pallas as pl
from jax.experimental.pallas import tpu as pltpu

HEAD_DIM = 128
GRID_W = 64
ROPE_THETA = 10000.0
EPS = 1e-6
A_HEADS = 8
A_KV_HEADS = 2
WINDOW = 128
B_HEADS = 8
B_KV_HEADS = 2
C_HEADS = 8
NB_ROWS = 8
NB_COLS = 16
N_EXPERTS = 16
CAPACITY_FACTOR = 2
NEG_INF = -1e30

A_Q = A_HEADS * HEAD_DIM
A_KV = A_KV_HEADS * HEAD_DIM
B_Q = B_HEADS * HEAD_DIM
B_KV = B_KV_HEADS * HEAD_DIM
C_W = C_HEADS * HEAD_DIM
QKV_W = A_Q + 2 * A_KV + B_Q + 2 * B_KV + 3 * C_W
QA_H, KA_H, VA_H = 0, 8, 10
QB_H, KB_H, VB_H = 12, 20, 22
QC_H, KC_H, VC_H = 24, 32, 40
QK_SCALE = 1.0 / math.sqrt(HEAD_DIM)

VMEM_LIMIT_BYTES = 52 * 1024 * 1024
ROUTER_LANES = 128

BF16 = jnp.bfloat16
F32 = jnp.float32


def _cparams(*sem):
    return pltpu.CompilerParams(dimension_semantics=sem, vmem_limit_bytes=VMEM_LIMIT_BYTES)


def _rms_rows(x, g):
    ms = jnp.mean(x * x, axis=-1, keepdims=True)
    return x * lax.rsqrt(ms + EPS) * g


def _qkv_kernel(x_ref, g_ref, w_ref, qkn_ref, c1_ref, s1_ref, ca_ref, sa_ref, sb_ref, o_ref, h_ref, *,
                heads_per_tile):
    j = pl.program_id(1)

    @pl.when(j == 0)
    def _():
        h_ref[...] = _rms_rows(x_ref[...], g_ref[...]).astype(BF16)

    acc = jnp.dot(h_ref[...], w_ref[...], preferred_element_type=F32)

    def heads(fn):
        for t in range(heads_per_tile):
            sl = slice(t * HEAD_DIM, (t + 1) * HEAD_DIM)
            o_ref[:, sl] = fn(acc[:, sl]).astype(o_ref.dtype)

    def norm(a, gi, scale):
        return _rms_rows(a, qkn_ref[gi:gi + 1, :] * scale)

    def rope1d(y):
        return y * c1_ref[...] + pltpu.roll(y, 64, 1) * s1_ref[...]

    def axial(y):
        return y * ca_ref[...] + pltpu.roll(y, 96, 1) * sa_ref[...] + pltpu.roll(y, 32, 1) * sb_ref[...]

    hpt = heads_per_tile
    segs = [
        (QA_H, KA_H, lambda a: rope1d(norm(a, 0, QK_SCALE))),
        (KA_H, VA_H, lambda a: rope1d(norm(a, 1, 1.0))),
        (VA_H, QB_H, lambda a: a),
        (QB_H, KB_H, lambda a: axial(norm(a, 2, QK_SCALE))),
        (KB_H, VB_H, lambda a: axial(norm(a, 3, 1.0))),
        (VB_H, QC_H, lambda a: a),
        (QC_H, KC_H, lambda a: norm(a, 4, QK_SCALE)),
        (KC_H, VC_H, lambda a: norm(a, 5, 1.0)),
        (VC_H, VC_H + C_HEADS, lambda a: a),
    ]
    for lo, hi, fn in segs:
        assert lo % hpt == 0 and hi % hpt == 0
        pl.when((j >= lo // hpt) & (j < hi // hpt))(functools.partial(heads, fn))


def _qkv_proj(x2d, norm_g, w_qkv, qkn, tabs, seq_len, tm, tn=256):
    T, D = x2d.shape
    nseq = seq_len // tm
    tab_spec = pl.BlockSpec((tm, HEAD_DIM), lambda i, j: (i % nseq, 0))
    return pl.pallas_call(
        functools.partial(_qkv_kernel, heads_per_tile=tn // HEAD_DIM),
        grid=(T // tm, QKV_W // tn),
        in_specs=[
            pl.BlockSpec((tm, D), lambda i, j: (i, 0)),
            pl.BlockSpec((1, D), lambda i, j: (0, 0)),
            pl.BlockSpec((D, tn), lambda i, j: (0, j)),
            pl.BlockSpec((8, HEAD_DIM), lambda i, j: (0, 0)),
        ] + [tab_spec] * 5,
        out_specs=pl.BlockSpec((tm, tn), lambda i, j: (i, j)),
        out_shape=jax.ShapeDtypeStruct((T, QKV_W), BF16),
        scratch_shapes=[pltpu.VMEM((tm, D), BF16)],
        compiler_params=_cparams("parallel", "arbitrary"),
        name="qkv_proj",
    )(x2d, norm_g, w_qkv, qkn, *tabs)


def _rope_tables(seq_len):
    def cos_sin(pos, dim):
        inv = ROPE_THETA ** (-jnp.arange(0, dim, 2, dtype=F32) / dim)
        ang = pos.astype(F32)[:, None] * inv[None, :]
        return jnp.cos(ang), jnp.sin(ang)

    t = jnp.arange(seq_len, dtype=jnp.int32)
    c, s = cos_sin(t, HEAD_DIM)
    cr, sr = cos_sin(t // GRID_W, HEAD_DIM // 2)
    cc, sc = cos_sin(t % GRID_W, HEAD_DIM // 2)
    z = jnp.zeros_like(sr)
    c1 = jnp.concatenate([c, c], axis=-1)
    s1 = jnp.concatenate([-s, s], axis=-1)
    ca = jnp.concatenate([cr, cr, cc, cc], axis=-1)
    sa = jnp.concatenate([-sr, z, -sc, z], axis=-1)
    sb = jnp.concatenate([z, sr, z, sc], axis=-1)
    return c1, s1, ca, sa, sb


def _attn_a_kernel(sink_ref, q_ref, k_ref, v_ref, o_ref, *, tq, tk, seq_len):
    g = pl.program_id(1)
    qi = pl.program_id(2)
    q0 = qi * tq
    start = jnp.clip(q0 - WINDOW, 0, seq_len - tk)
    start = pl.multiple_of(start, WINDOW)
    k = k_ref[0, pl.ds(start, tk), :]
    v = v_ref[0, pl.ds(start, tk), :]
    qpos = q0 + lax.broadcasted_iota(jnp.int32, (tq, tk), 0)
    kpos = start + lax.broadcasted_iota(jnp.int32, (tq, tk), 1)
    valid = jnp.abs(qpos - kpos) <= WINDOW
    group = A_HEADS // A_KV_HEADS
    for h in range(group):
        sl = slice(h * HEAD_DIM, (h + 1) * HEAD_DIM)
        s = lax.dot_general(q_ref[0, :, sl], k, (((1,), (1,)), ((), ())), preferred_element_type=F32)
        s = jnp.where(valid, s, NEG_INF)
        sink = sink_ref[g * group + h]
        m = jnp.maximum(jnp.max(s, axis=-1, keepdims=True), sink)
        p = jnp.exp(s - m)
        denom = jnp.sum(p, axis=-1, keepdims=True) + jnp.exp(sink - m)
        o = jnp.dot(p.astype(BF16), v, preferred_element_type=F32)
        o_ref[0, :, sl] = (o / denom).astype(o_ref.dtype)


def _attn_a(qkv, sink, tq=256):
    B, S, _ = qkv.shape
    tk = tq + 2 * WINDOW
    gw = (A_HEADS // A_KV_HEADS) * HEAD_DIM
    return pl.pallas_call(
        functools.partial(_attn_a_kernel, tq=tq, tk=tk, seq_len=S),
        grid_spec=pltpu.PrefetchScalarGridSpec(
            num_scalar_prefetch=1,
            grid=(B, A_KV_HEADS, S // tq),
            in_specs=[
                pl.BlockSpec((1, tq, gw), lambda b, g, i, s: (b, i, QA_H * HEAD_DIM // gw + g)),
                pl.BlockSpec((1, S, HEAD_DIM), lambda b, g, i, s: (b, 0, KA_H + g)),
                pl.BlockSpec((1, S, HEAD_DIM), lambda b, g, i, s: (b, 0, VA_H + g)),
            ],
            out_specs=pl.BlockSpec((1, tq, gw), lambda b, g, i, s: (b, i, g)),
        ),
        out_shape=jax.ShapeDtypeStruct((B, S, A_Q), BF16),
        compiler_params=_cparams("parallel", "parallel", "arbitrary"),
        name="attn_window",
    )(sink, qkv, qkv, qkv)


def _attn_b_kernel(q_ref, k_ref, v_ref, o_ref, m_ref, l_ref, acc_ref):
    ki = pl.program_id(3)

    @pl.when(ki == 0)
    def _():
        m_ref[...] = jnp.full(m_ref.shape, -jnp.inf, F32)
        l_ref[...] = jnp.zeros(l_ref.shape, F32)
        acc_ref[...] = jnp.zeros(acc_ref.shape, F32)

    k = k_ref[0]
    v = v_ref[0]
    group = B_HEADS // B_KV_HEADS
    for h in range(group):
        sl = slice(h * HEAD_DIM, (h + 1) * HEAD_DIM)
        s = lax.dot_general(q_ref[0, :, sl], k, (((1,), (1,)), ((), ())), preferred_element_type=F32)
        m_prev = m_ref[h]
        m_cur = jnp.maximum(m_prev, jnp.max(s, axis=-1, keepdims=True))
        alpha = jnp.exp(m_prev - m_cur)
        p = jnp.exp(s - m_cur)
        l_ref[h] = alpha * l_ref[h] + jnp.sum(p, axis=-1, keepdims=True)
        acc_ref[h] = alpha * acc_ref[h] + jnp.dot(p.astype(BF16), v, preferred_element_type=F32)
        m_ref[h] = m_cur

    @pl.when(ki == pl.num_programs(3) - 1)
    def _():
        for h in range(group):
            sl = slice(h * HEAD_DIM, (h + 1) * HEAD_DIM)
            o_ref[0, :, sl] = (acc_ref[h] / l_ref[h]).astype(o_ref.dtype)


def _attn_b(qkv, tq=512, tk=1024):
    B, S, _ = qkv.shape
    tq, tk = min(tq, S), min(tk, S)
    group = B_HEADS // B_KV_HEADS
    gw = group * HEAD_DIM
    return pl.pallas_call(
        _attn_b_kernel,
        grid=(B, B_KV_HEADS, S // tq, S // tk),
        in_specs=[
            pl.BlockSpec((1, tq, gw), lambda b, g, i, k: (b, i, QB_H * HEAD_DIM // gw + g)),
            pl.BlockSpec((1, tk, HEAD_DIM), lambda b, g, i, k: (b, k, KB_H + g)),
            pl.BlockSpec((1, tk, HEAD_DIM), lambda b, g, i, k: (b, k, VB_H + g)),
        ],
        out_specs=pl.BlockSpec((1, tq, gw), lambda b, g, i, k: (b, i, g)),
        out_shape=jax.ShapeDtypeStruct((B, S, B_Q), BF16),
        scratch_shapes=[
            pltpu.VMEM((group, tq, 1), F32),
            pltpu.VMEM((group, tq, 1), F32),
            pltpu.VMEM((group, tq, HEAD_DIM), F32),
        ],
        compiler_params=_cparams("parallel", "parallel", "parallel", "arbitrary"),
        name="attn_dense",
    )(qkv, qkv, qkv)


C_QROWS = 8
C_KROWS = 16
C_TQ = C_QROWS * GRID_W
C_TK = C_KROWS * GRID_W


def _attn_c_kernel(q_ref, k_ref, v_ref, bias_ref, o_ref, *, rows):
    i = pl.program_id(2)
    start_row = jnp.clip(i * C_QROWS - NB_ROWS // 2, 0, rows - C_KROWS)
    start = pl.multiple_of(start_row * GRID_W, (NB_ROWS // 2) * GRID_W)
    k = k_ref[0, pl.ds(start, C_TK), :]
    v = v_ref[0, pl.ds(start, C_TK), :]
    s = lax.dot_general(q_ref[0], k, (((1,), (1,)), ((), ())), preferred_element_type=F32)
    s = s + bias_ref[0, 0]
    m = jnp.max(s, axis=-1, keepdims=True)
    p = jnp.exp(s - m)
    denom = jnp.sum(p, axis=-1, keepdims=True)
    o = jnp.dot(p.astype(BF16), v, preferred_element_type=F32)
    o_ref[0] = (o / denom).astype(o_ref.dtype)


def _nbr_bias_tables(rpb, rows):
    assert rows % C_QROWS == 0 and rows >= C_KROWS
    rpb = rpb.astype(F32)
    dr = jnp.arange(C_QROWS)[:, None, None, None]
    c = jnp.arange(GRID_W)[None, :, None, None]
    kj = jnp.arange(C_KROWS)[None, None, :, None]
    kc = jnp.arange(GRID_W)[None, None, None, :]
    tables = []
    for first_row in (0, min(C_QROWS, rows - C_KROWS + NB_ROWS // 2), rows - C_QROWS):
        start = min(max(first_row - NB_ROWS // 2, 0), rows - C_KROWS)
        r = first_row + dr
        rs = jnp.clip(r - NB_ROWS // 2, 0, rows - NB_ROWS)
        cs = jnp.clip(c - NB_COLS // 2, 0, GRID_W - NB_COLS)
        kr = start + kj
        valid = (kr >= rs) & (kr < rs + NB_ROWS) & (kc >= cs) & (kc < cs + NB_COLS)
        row_off = jnp.clip(kr - r + (NB_ROWS - 1), 0, 2 * NB_ROWS - 2)
        col_off = jnp.clip(kc - c + (NB_COLS - 1), 0, 2 * NB_COLS - 2)
        bias = rpb[:, row_off, col_off]
        bias = jnp.where(valid[None], bias, NEG_INF)
        tables.append(bias.reshape(C_HEADS, C_TQ, C_TK))
    return jnp.stack(tables)


def _attn_c(qkv, bias_tables):
    B, S, _ = qkv.shape
    rows = S // GRID_W
    n = S // C_TQ

    def bias_idx(h, b, i):
        return (jnp.where(i == 0, 0, jnp.where(i == n - 1, 2, 1)), h, 0, 0)

    return pl.pallas_call(
        functools.partial(_attn_c_kernel, rows=rows),
        grid=(C_HEADS, B, n),
        in_specs=[
            pl.BlockSpec((1, C_TQ, HEAD_DIM), lambda h, b, i: (b, i, QC_H + h)),
            pl.BlockSpec((1, S, HEAD_DIM), lambda h, b, i: (b, 0, KC_H + h)),
            pl.BlockSpec((1, S, HEAD_DIM), lambda h, b, i: (b, 0, VC_H + h)),
            pl.BlockSpec((1, 1, C_TQ, C_TK), bias_idx),
        ],
        out_specs=pl.BlockSpec((1, C_TQ, HEAD_DIM), lambda h, b, i: (b, i, h)),
        out_shape=jax.ShapeDtypeStruct((B, S, C_W), BF16),
        compiler_params=_cparams("parallel", "parallel", "arbitrary"),
        name="attn_nbr",
    )(qkv, qkv, qkv, bias_tables)


def _merge_kernel(x_ref, g_ref, oa_ref, ob_ref, oc_ref, wga_ref, wgb_ref, wgc_ref, wa_ref, wb_ref, wc_ref,
                  o_ref, h_ref):
    @pl.when(pl.program_id(1) == 0)
    def _():
        h_ref[...] = _rms_rows(x_ref[...], g_ref[...]).astype(BF16)

    h = h_ref[...]
    merged = None
    for o_r, wg_r, w_r in ((oa_ref, wga_ref, wa_ref), (ob_ref, wgb_ref, wb_ref), (oc_ref, wgc_ref, wc_ref)):
        gate = jnp.dot(h, wg_r[...], preferred_element_type=F32)
        y = jnp.dot(o_r[...], w_r[...], preferred_element_type=F32)
        term = jax.nn.sigmoid(gate) * y
        merged = term if merged is None else merged + term
    o_ref[...] = merged.astype(o_ref.dtype)


def _merge(x2d, norm_g, oa, ob, oc, w_gates, wba, wbb, wbc, tm, tn=256):
    T, D = x2d.shape
    nj = D // tn
    o_spec = lambda w: pl.BlockSpec((tm, w), lambda i, j: (i, 0))
    wg_spec = lambda k: pl.BlockSpec((D, tn), lambda i, j: (0, k * nj + j))
    wb_spec = lambda w: pl.BlockSpec((w, tn), lambda i, j: (0, j))
    return pl.pallas_call(
        _merge_kernel,
        grid=(T // tm, nj),
        in_specs=[
            pl.BlockSpec((tm, D), lambda i, j: (i, 0)),
            pl.BlockSpec((1, D), lambda i, j: (0, 0)),
            o_spec(A_Q), o_spec(B_Q), o_spec(C_W),
            wg_spec(0), wg_spec(1), wg_spec(2),
            wb_spec(A_Q), wb_spec(B_Q), wb_spec(C_W),
        ],
        out_specs=pl.BlockSpec((tm, tn), lambda i, j: (i, j)),
        out_shape=jax.ShapeDtypeStruct((T, D), BF16),
        scratch_shapes=[pltpu.VMEM((tm, D), BF16)],
        compiler_params=_cparams("parallel", "arbitrary"),
        name="gated_merge",
    )(x2d, norm_g, oa, ob, oc, w_gates, w_gates, w_gates, wba, wbb, wbc)


def _outproj_kernel(x_ref, m_ref, w_ref, g_ref, wr_ref, x1_ref, h2_ref, lg_ref):
    x1 = x_ref[...] + jnp.dot(m_ref[...], w_ref[...], preferred_element_type=F32)
    x1_ref[...] = x1
    h2 = _rms_rows(x1, g_ref[...])
    hi = h2.astype(BF16)
    lo = (h2 - hi.astype(F32)).astype(BF16)
    h2_ref[...] = hi
    wr = wr_ref[...]
    r = jnp.dot(hi, wr, preferred_element_type=F32) + jnp.dot(lo, wr, preferred_element_type=F32)
    lg_ref[...] = r + pltpu.roll(r, ROUTER_LANES - N_EXPERTS, 1)


def _outproj(x2d, merged, w_out, norm_g, wr_cat, tm):
    T, D = x2d.shape
    return pl.pallas_call(
        _outproj_kernel,
        grid=(T // tm,),
        in_specs=[
            pl.BlockSpec((tm, D), lambda i: (i, 0)),
            pl.BlockSpec((tm, D), lambda i: (i, 0)),
            pl.BlockSpec((D, D), lambda i: (0, 0)),
            pl.BlockSpec((1, D), lambda i: (0, 0)),
            pl.BlockSpec((D, ROUTER_LANES), lambda i: (0, 0)),
        ],
        out_specs=[
            pl.BlockSpec((tm, D), lambda i: (i, 0)),
            pl.BlockSpec((tm, D), lambda i: (i, 0)),
            pl.BlockSpec((tm, ROUTER_LANES), lambda i: (i, 0)),
        ],
        out_shape=[
            jax.ShapeDtypeStruct((T, D), F32),
            jax.ShapeDtypeStruct((T, D), BF16),
            jax.ShapeDtypeStruct((T, ROUTER_LANES), F32),
        ],
        compiler_params=_cparams("parallel"),
        name="out_proj",
    )(x2d, merged, w_out, norm_g, wr_cat)


def _router_split(w_router):
    hi = w_router.astype(BF16)
    lo = (w_router - hi.astype(F32)).astype(BF16)
    pad = jnp.zeros((w_router.shape[0], ROUTER_LANES - 2 * N_EXPERTS), BF16)
    return jnp.concatenate([hi, lo, pad], axis=1)


def _expert_kernel(xs_ref, wg_ref, wu_ref, wd_ref, gs_ref, o_ref):
    f = pl.program_id(2)
    xs = xs_ref[0]
    a = jnp.dot(xs, wg_ref[0], preferred_element_type=F32)
    u = jnp.dot(xs, wu_ref[0], preferred_element_type=F32)
    hmid = (jax.nn.silu(a) * u).astype(BF16)
    part = jnp.dot(hmid, wd_ref[0], preferred_element_type=F32)

    @pl.when(f == 0)
    def _():
        o_ref[0] = part

    @pl.when(f > 0)
    def _():
        o_ref[0] += part

    @pl.when(f == pl.num_programs(2) - 1)
    def _():
        o_ref[0] *= gs_ref[0]


def _experts(xs, w_gate, w_up, w_down, gscale, tm=1024, tf=512):
    E, C, D = xs.shape
    F = w_gate.shape[2]
    tm, tf = min(tm, C), min(tf, F)
    return pl.pallas_call(
        _expert_kernel,
        grid=(E, C // tm, F // tf),
        in_specs=[
            pl.BlockSpec((1, tm, D), lambda e, m, f: (e, m, 0)),
            pl.BlockSpec((1, D, tf), lambda e, m, f: (e, 0, f)),
            pl.BlockSpec((1, D, tf), lambda e, m, f: (e, 0, f)),
            pl.BlockSpec((1, tf, D), lambda e, m, f: (e, f, 0)),
            pl.BlockSpec((1, tm, 1), lambda e, m, f: (e, m, 0)),
        ],
        out_specs=pl.BlockSpec((1, tm, D), lambda e, m, f: (e, m, 0)),
        out_shape=jax.ShapeDtypeStruct((E, C, D), F32),
        compiler_params=_cparams("parallel", "parallel", "arbitrary"),
        name="expert_swiglu",
    )(xs, w_gate, w_up, w_down, gscale)


def _layer(x, lw, tabs, bias_tables):
    B, S, D = x.shape
    T = B * S
    tm = min(512, S)
    x2d = x.reshape(T, D)
    qkv = _qkv_proj(x2d, lw["norm_attn"], lw["w_qkv"], lw["qkn"], tabs, S, tm).reshape(B, S, QKV_W)
    oa = _attn_a(qkv, lw["sink_a"]).reshape(T, A_Q)
    ob = _attn_b(qkv).reshape(T, B_Q)
    oc = _attn_c(qkv, bias_tables).reshape(T, C_W)
    merged = _merge(x2d, lw["norm_attn"], oa, ob, oc, lw["w_gates"], lw["wba"], lw["wbb"], lw["wbc"], tm)
    x1, h2, logits = _outproj(x2d, merged, lw["w_out"], lw["norm_ffn"], lw["wr_cat"], min(256, S))

    cap = CAPACITY_FACTOR * T // N_EXPERTS
    aff = jax.nn.softmax(logits[:, :N_EXPERTS], axis=-1)
    gsel, idx = lax.top_k(aff.T, cap)
    xs = h2[idx]
    ye = _experts(xs, lw["w_gate"], lw["w_up"], lw["w_down"], gsel[..., None])
    y = jnp.zeros((T, D), F32).at[idx.reshape(-1)].add(ye.reshape(-1, D))
    return (x1 + y).reshape(B, S, D)


def kernel(x_prompt, x_sample, norm_attn, w_in, qk_norm, sink_a, rpb_c, w_branch_a, w_branch_b, w_branch_c,
           w_out, norm_ffn, w_router, w_gate, w_up, w_down):
    depth = w_in.shape[0]
    layers = []
    for l in range(depth):
        layers.append(dict(
            norm_attn=norm_attn[l][None, :],
            w_qkv=w_in[l, :, :QKV_W].astype(BF16),
            w_gates=w_in[l, :, QKV_W:].astype(BF16),
            qkn=jnp.pad(qk_norm[l], ((0, 2), (0, 0))),
            sink_a=sink_a[l],
            rpb=rpb_c[l],
            wba=w_branch_a[l].astype(BF16),
            wbb=w_branch_b[l].astype(BF16),
            wbc=w_branch_c[l].astype(BF16),
            w_out=w_out[l].astype(BF16),
            norm_ffn=norm_ffn[l][None, :],
            wr_cat=_router_split(w_router[l]),
            w_gate=w_gate[l].astype(BF16),
            w_up=w_up[l].astype(BF16),
            w_down=w_down[l].astype(BF16),
        ))

    def trunk(x):
        S = x.shape[1]
        tabs = _rope_tables(S)
        for lw in layers:
            x = _layer(x, lw, tabs, _nbr_bias_tables(lw["rpb"], S // GRID_W))
        return x

    return trunk(x_prompt), trunk(x_sample)
```

```python
import functools
import math

import jax
import jax.numpy as jnp
from jax import lax
from jax.experimental import pallas as pl
from jax.experimental.pallas import tpu as pltpu

HEAD_DIM = 128
GRID_W = 64
ROPE_THETA = 10000.0
EPS = 1e-6
A_HEADS = 8
A_KV_HEADS = 2
WINDOW = 128
B_HEADS = 8
B_KV_HEADS = 2
C_HEADS = 8
NB_ROWS = 8
NB_COLS = 16
N_EXPERTS = 16
CAPACITY_FACTOR = 2
NEG_INF = -1e30

A_Q = A_HEADS * HEAD_DIM
A_KV = A_KV_HEADS * HEAD_DIM
B_Q = B_HEADS * HEAD_DIM
B_KV = B_KV_HEADS * HEAD_DIM
C_W = C_HEADS * HEAD_DIM
QKV_W = A_Q + 2 * A_KV + B_Q + 2 * B_KV + 3 * C_W
QA_H, KA_H, VA_H = 0, 8, 10
QB_H, KB_H, VB_H = 12, 20, 22
QC_H, KC_H, VC_H = 24, 32, 40
QK_SCALE = 1.0 / math.sqrt(HEAD_DIM)
LOG2E = math.log2(math.e)

VMEM_LIMIT_BYTES = 52 * 1024 * 1024
ROUTER_LANES = 128
MATMUL_TN = 256

BF16 = jnp.bfloat16
F32 = jnp.float32


def _cparams(*sem):
    return pltpu.CompilerParams(dimension_semantics=sem, vmem_limit_bytes=VMEM_LIMIT_BYTES)


def _rms_rows(x, g):
    ms = jnp.mean(x * x, axis=-1, keepdims=True)
    return x * lax.rsqrt(ms + EPS) * g


def _qkv_kernel(x_ref, g_ref, w_ref, qkn_ref, c1_ref, s1_ref, ca_ref, sa_ref, sb_ref, o_ref, h_ref, *,
                heads_per_tile):
    j = pl.program_id(1)

    @pl.when(j == 0)
    def _():
        h_ref[...] = _rms_rows(x_ref[...], g_ref[...]).astype(BF16)

    acc = jnp.dot(h_ref[...], w_ref[0], preferred_element_type=F32)

    def heads(fn):
        for t in range(heads_per_tile):
            sl = slice(t * HEAD_DIM, (t + 1) * HEAD_DIM)
            o_ref[:, sl] = fn(acc[:, sl]).astype(o_ref.dtype)

    def norm(a, gi, scale):
        return _rms_rows(a, qkn_ref[gi:gi + 1, :] * scale)

    def rope1d(y):
        return y * c1_ref[...] + pltpu.roll(y, 64, 1) * s1_ref[...]

    def axial(y):
        return y * ca_ref[...] + pltpu.roll(y, 96, 1) * sa_ref[...] + pltpu.roll(y, 32, 1) * sb_ref[...]

    hpt = heads_per_tile
    segs = [
        (QA_H, KA_H, lambda a: rope1d(norm(a, 0, QK_SCALE))),
        (KA_H, VA_H, lambda a: rope1d(norm(a, 1, 1.0))),
        (VA_H, QB_H, lambda a: a),
        (QB_H, KB_H, lambda a: axial(norm(a, 2, QK_SCALE * LOG2E))),
        (KB_H, VB_H, lambda a: axial(norm(a, 3, 1.0))),
        (VB_H, QC_H, lambda a: a),
        (QC_H, KC_H, lambda a: norm(a, 4, QK_SCALE)),
        (KC_H, VC_H, lambda a: norm(a, 5, 1.0)),
        (VC_H, VC_H + C_HEADS, lambda a: a),
    ]
    for lo, hi, fn in segs:
        assert lo % hpt == 0 and hi % hpt == 0
        pl.when((j >= lo // hpt) & (j < hi // hpt))(functools.partial(heads, fn))


def _col_tiles(w, tn):
    K, N = w.shape
    return w.reshape(K, N // tn, tn).transpose(1, 0, 2)


def _qkv_proj(x2d, norm_g, w_qkv, qkn, tabs, seq_len, tm):
    T, D = x2d.shape
    tn = w_qkv.shape[2]
    nseq = seq_len // tm
    tab_spec = pl.BlockSpec((tm, HEAD_DIM), lambda i, j: (i % nseq, 0))
    return pl.pallas_call(
        functools.partial(_qkv_kernel, heads_per_tile=tn // HEAD_DIM),
        grid=(T // tm, QKV_W // tn),
        in_specs=[
            pl.BlockSpec((tm, D), lambda i, j: (i, 0)),
            pl.BlockSpec((1, D), lambda i, j: (0, 0)),
            pl.BlockSpec((1, D, tn), lambda i, j: (j, 0, 0)),
            pl.BlockSpec((8, HEAD_DIM), lambda i, j: (0, 0)),
        ] + [tab_spec] * 5,
        out_specs=pl.BlockSpec((tm, tn), lambda i, j: (i, j)),
        out_shape=jax.ShapeDtypeStruct((T, QKV_W), BF16),
        scratch_shapes=[pltpu.VMEM((tm, D), BF16)],
        compiler_params=_cparams("parallel", "arbitrary"),
        name="qkv_proj",
    )(x2d, norm_g, w_qkv, qkn, *tabs)


def _rope_tables(seq_len):
    def cos_sin(pos, dim):
        inv = ROPE_THETA ** (-jnp.arange(0, dim, 2, dtype=F32) / dim)
        ang = pos.astype(F32)[:, None] * inv[None, :]
        return jnp.cos(ang), jnp.sin(ang)

    t = jnp.arange(seq_len, dtype=jnp.int32)
    c, s = cos_sin(t, HEAD_DIM)
    cr, sr = cos_sin(t // GRID_W, HEAD_DIM // 2)
    cc, sc = cos_sin(t % GRID_W, HEAD_DIM // 2)
    z = jnp.zeros_like(sr)
    c1 = jnp.concatenate([c, c], axis=-1)
    s1 = jnp.concatenate([-s, s], axis=-1)
    ca = jnp.concatenate([cr, cr, cc, cc], axis=-1)
    sa = jnp.concatenate([-sr, z, -sc, z], axis=-1)
    sb = jnp.concatenate([z, sr, z, sc], axis=-1)
    return c1, s1, ca, sa, sb


def _attn_a_kernel(sink_ref, q_ref, k_ref, vt_ref, o_ref, *, tq, tk, seq_len):
    g = pl.program_id(1)
    qi = pl.program_id(2)
    q0 = qi * tq
    start = jnp.clip(q0 - WINDOW, 0, seq_len - tk)
    start = pl.multiple_of(start, WINDOW)
    k = k_ref[0, pl.ds(start, tk), :]
    vt = vt_ref[0, :, pl.ds(start, tk)]
    kpos = start + lax.broadcasted_iota(jnp.int32, (tk, tq), 0)
    qpos = q0 + lax.broadcasted_iota(jnp.int32, (tk, tq), 1)
    valid = jnp.abs(qpos - kpos) <= WINDOW
    group = A_HEADS // A_KV_HEADS
    for h in range(group):
        sl = slice(h * HEAD_DIM, (h + 1) * HEAD_DIM)
        st = lax.dot_general(k, q_ref[0, :, sl], (((1,), (1,)), ((), ())), preferred_element_type=F32)
        st = jnp.where(valid, st, NEG_INF)
        sink = sink_ref[g * group + h]
        m = jnp.maximum(jnp.max(st, axis=0, keepdims=True), sink)
        pt = jnp.exp(st - m)
        denom = jnp.sum(pt, axis=0, keepdims=True) + jnp.exp(sink - m)
        ot = jnp.dot(vt, pt.astype(BF16), preferred_element_type=F32)
        o_ref[0, :, sl] = (ot / denom).T.astype(o_ref.dtype)


def _attn_a(qkv, sink, tq=256):
    B, S, _ = qkv.shape
    tk = tq + 2 * WINDOW
    gw = (A_HEADS // A_KV_HEADS) * HEAD_DIM
    vt = jnp.swapaxes(qkv[:, :, VA_H * HEAD_DIM:(VA_H + A_KV_HEADS) * HEAD_DIM], 1, 2)
    return pl.pallas_call(
        functools.partial(_attn_a_kernel, tq=tq, tk=tk, seq_len=S),
        grid_spec=pltpu.PrefetchScalarGridSpec(
            num_scalar_prefetch=1,
            grid=(B, A_KV_HEADS, S // tq),
            in_specs=[
                pl.BlockSpec((1, tq, gw), lambda b, g, i, s: (b, i, QA_H * HEAD_DIM // gw + g)),
                pl.BlockSpec((1, S, HEAD_DIM), lambda b, g, i, s: (b, 0, KA_H + g)),
                pl.BlockSpec((1, HEAD_DIM, S), lambda b, g, i, s: (b, g, 0)),
            ],
            out_specs=pl.BlockSpec((1, tq, gw), lambda b, g, i, s: (b, i, g)),
        ),
        out_shape=jax.ShapeDtypeStruct((B, S, A_Q), BF16),
        compiler_params=_cparams("parallel", "parallel", "arbitrary"),
        name="attn_window",
    )(sink, qkv, qkv, vt)


def _attn_b_kernel(q_ref, k_ref, vt_ref, o_ref, m_ref, l_ref, acc_ref):
    ki = pl.program_id(3)

    @pl.when(ki == 0)
    def _():
        m_ref[...] = jnp.full(m_ref.shape, -jnp.inf, F32)
        l_ref[...] = jnp.zeros(l_ref.shape, F32)
        acc_ref[...] = jnp.zeros(acc_ref.shape, F32)

    k = k_ref[0]
    vt = vt_ref[0]
    group = B_HEADS // B_KV_HEADS
    for h in range(group):
        sl = slice(h * HEAD_DIM, (h + 1) * HEAD_DIM)
        st = lax.dot_general(k, q_ref[0, :, sl], (((1,), (1,)), ((), ())), preferred_element_type=F32)
        m_prev = m_ref[h]
        m_cur = jnp.maximum(m_prev, jnp.max(st, axis=0, keepdims=True))
        alpha = jnp.exp2(m_prev - m_cur)
        pt = jnp.exp2(st - m_cur)
        l_ref[h] = alpha * l_ref[h] + jnp.sum(pt, axis=0, keepdims=True)
        acc_ref[h] = alpha * acc_ref[h] + jnp.dot(vt, pt.astype(BF16), preferred_element_type=F32)
        m_ref[h] = m_cur

    @pl.when(ki == pl.num_programs(3) - 1)
    def _():
        for h in range(group):
            sl = slice(h * HEAD_DIM, (h + 1) * HEAD_DIM)
            o_ref[0, :, sl] = (acc_ref[h] / l_ref[h]).T.astype(o_ref.dtype)


def _attn_b(qkv, tq=512, tk=4096):
    B, S, _ = qkv.shape
    tq, tk = min(tq, S), min(tk, S)
    group = B_HEADS // B_KV_HEADS
    gw = group * HEAD_DIM
    vt = jnp.swapaxes(qkv[:, :, VB_H * HEAD_DIM:(VB_H + B_KV_HEADS) * HEAD_DIM], 1, 2)
    return pl.pallas_call(
        _attn_b_kernel,
        grid=(B, B_KV_HEADS, S // tq, S // tk),
        in_specs=[
            pl.BlockSpec((1, tq, gw), lambda b, g, i, k: (b, i, QB_H * HEAD_DIM // gw + g)),
            pl.BlockSpec((1, tk, HEAD_DIM), lambda b, g, i, k: (b, k, KB_H + g)),
            pl.BlockSpec((1, HEAD_DIM, tk), lambda b, g, i, k: (b, g, k)),
        ],
        out_specs=pl.BlockSpec((1, tq, gw), lambda b, g, i, k: (b, i, g)),
        out_shape=jax.ShapeDtypeStruct((B, S, B_Q), BF16),
        scratch_shapes=[
            pltpu.VMEM((group, 1, tq), F32),
            pltpu.VMEM((group, 1, tq), F32),
            pltpu.VMEM((group, HEAD_DIM, tq), F32),
        ],
        compiler_params=_cparams("parallel", "parallel", "parallel", "arbitrary"),
        name="attn_dense",
    )(qkv, qkv, vt)


C_QROWS = 8
C_KROWS = 16
C_TQ = C_QROWS * GRID_W
C_TK = C_KROWS * GRID_W


def _attn_c_kernel(q_ref, k_ref, vt_ref, bias_ref, o_ref, *, rows):
    i = pl.program_id(2)
    start_row = jnp.clip(i * C_QROWS - NB_ROWS // 2, 0, rows - C_KROWS)
    start = pl.multiple_of(start_row * GRID_W, (NB_ROWS // 2) * GRID_W)
    k = k_ref[0, pl.ds(start, C_TK), :]
    vt = vt_ref[0, :, pl.ds(start, C_TK)]
    st = lax.dot_general(k, q_ref[0], (((1,), (1,)), ((), ())), preferred_element_type=F32)
    st = st + bias_ref[0, 0]
    pt = jnp.exp(st - jnp.max(st, axis=0, keepdims=True))
    denom = jnp.sum(pt, axis=0, keepdims=True)
    ot = jnp.dot(vt, pt.astype(BF16), preferred_element_type=F32)
    o_ref[0] = (ot / denom).T.astype(o_ref.dtype)


def _nbr_bias_tables(rpb, rows):
    assert rows % C_QROWS == 0 and rows >= C_KROWS
    rpb = rpb.astype(F32)
    c = jnp.arange(GRID_W)[:, None]
    kc = jnp.arange(GRID_W)[None, :]
    cs = jnp.clip(c - NB_COLS // 2, 0, GRID_W - NB_COLS)
    col_valid = (kc >= cs) & (kc < cs + NB_COLS)
    col_hot = (kc - c + (NB_COLS - 1))[..., None] == jnp.arange(2 * NB_COLS - 1)
    col_hot = (col_hot & col_valid[..., None]).astype(F32)
    by_col = jnp.einsum("hrc,qkc->hrqk", rpb, col_hot, precision=lax.Precision.HIGHEST)
    tables = []
    for first_row in (0, min(C_QROWS, rows - C_KROWS + NB_ROWS // 2), rows - C_QROWS):
        start = min(max(first_row - NB_ROWS // 2, 0), rows - C_KROWS)
        r = first_row + jnp.arange(C_QROWS)[:, None]
        kr = start + jnp.arange(C_KROWS)[None, :]
        rs = jnp.clip(r - NB_ROWS // 2, 0, rows - NB_ROWS)
        row_valid = (kr >= rs) & (kr < rs + NB_ROWS)
        row_hot = (kr - r + (NB_ROWS - 1))[..., None] == jnp.arange(2 * NB_ROWS - 1)
        row_hot = (row_hot & row_valid[..., None]).astype(F32)
        bias = jnp.einsum("djr,hrqk->hjkdq", row_hot, by_col, precision=lax.Precision.HIGHEST)
        valid = row_valid.T[:, None, :, None] & col_valid.T[None, :, None, :]
        bias = jnp.where(valid[None], bias, NEG_INF)
        tables.append(bias.reshape(C_HEADS, C_TK, C_TQ))
    return jnp.stack(tables)


def _attn_c(qkv, bias_tables):
    B, S, _ = qkv.shape
    rows = S // GRID_W
    n = S // C_TQ
    vt = jnp.swapaxes(qkv[:, :, VC_H * HEAD_DIM:(VC_H + C_HEADS) * HEAD_DIM], 1, 2)

    def bias_idx(h, b, i):
        return (jnp.where(i == 0, 0, jnp.where(i == n - 1, 2, 1)), h, 0, 0)

    return pl.pallas_call(
        functools.partial(_attn_c_kernel, rows=rows),
        grid=(C_HEADS, B, n),
        in_specs=[
            pl.BlockSpec((1, C_TQ, HEAD_DIM), lambda h, b, i: (b, i, QC_H + h)),
            pl.BlockSpec((1, S, HEAD_DIM), lambda h, b, i: (b, 0, KC_H + h)),
            pl.BlockSpec((1, HEAD_DIM, S), lambda h, b, i: (b, h, 0)),
            pl.BlockSpec((1, 1, C_TK, C_TQ), bias_idx),
        ],
        out_specs=pl.BlockSpec((1, C_TQ, HEAD_DIM), lambda h, b, i: (b, i, h)),
        out_shape=jax.ShapeDtypeStruct((B, S, C_W), BF16),
        compiler_params=_cparams("parallel", "parallel", "arbitrary"),
        name="attn_nbr",
    )(qkv, qkv, vt, bias_tables)


def _merge_kernel(x_ref, g_ref, oa_ref, ob_ref, oc_ref, wga_ref, wgb_ref, wgc_ref, wa_ref, wb_ref, wc_ref,
                  o_ref, h_ref):
    @pl.when(pl.program_id(1) == 0)
    def _():
        h_ref[...] = _rms_rows(x_ref[...], g_ref[...]).astype(BF16)

    h = h_ref[...]
    merged = None
    for o_r, wg_r, w_r in ((oa_ref, wga_ref, wa_ref), (ob_ref, wgb_ref, wb_ref), (oc_ref, wgc_ref, wc_ref)):
        gate = jnp.dot(h, wg_r[0], preferred_element_type=F32)
        y = jnp.dot(o_r[...], w_r[0], preferred_element_type=F32)
        term = jax.nn.sigmoid(gate) * y
        merged = term if merged is None else merged + term
    o_ref[...] = merged.astype(o_ref.dtype)


def _merge(x2d, norm_g, oa, ob, oc, w_gates, wba, wbb, wbc, tm):
    T, D = x2d.shape
    tn = w_gates.shape[2]
    nj = D // tn
    o_spec = lambda w: pl.BlockSpec((tm, w), lambda i, j: (i, 0))
    wg_spec = lambda k: pl.BlockSpec((1, D, tn), lambda i, j: (k * nj + j, 0, 0))
    wb_spec = lambda w: pl.BlockSpec((1, w, tn), lambda i, j: (j, 0, 0))
    return pl.pallas_call(
        _merge_kernel,
        grid=(T // tm, nj),
        in_specs=[
            pl.BlockSpec((tm, D), lambda i, j: (i, 0)),
            pl.BlockSpec((1, D), lambda i, j: (0, 0)),
            o_spec(A_Q), o_spec(B_Q), o_spec(C_W),
            wg_spec(0), wg_spec(1), wg_spec(2),
            wb_spec(A_Q), wb_spec(B_Q), wb_spec(C_W),
        ],
        out_specs=pl.BlockSpec((tm, tn), lambda i, j: (i, j)),
        out_shape=jax.ShapeDtypeStruct((T, D), BF16),
        scratch_shapes=[pltpu.VMEM((tm, D), BF16)],
        compiler_params=_cparams("parallel", "arbitrary"),
        name="gated_merge",
    )(x2d, norm_g, oa, ob, oc, w_gates, w_gates, w_gates, wba, wbb, wbc)


def _outproj_kernel(x_ref, m_ref, w_ref, g_ref, wr_ref, x1_ref, h2_ref, lg_ref):
    x1 = x_ref[...] + jnp.dot(m_ref[...], w_ref[...], preferred_element_type=F32)
    x1_ref[...] = x1
    h2 = _rms_rows(x1, g_ref[...])
    hi = h2.astype(BF16)
    lo = (h2 - hi.astype(F32)).astype(BF16)
    h2_ref[...] = hi
    wr = wr_ref[...]
    r = jnp.dot(hi, wr, preferred_element_type=F32) + jnp.dot(lo, wr, preferred_element_type=F32)
    lg_ref[...] = r + pltpu.roll(r, ROUTER_LANES - N_EXPERTS, 1)


def _outproj(x2d, merged, w_out, norm_g, wr_cat, tm):
    T, D = x2d.shape
    return pl.pallas_call(
        _outproj_kernel,
        grid=(T // tm,),
        in_specs=[
            pl.BlockSpec((tm, D), lambda i: (i, 0)),
            pl.BlockSpec((tm, D), lambda i: (i, 0)),
            pl.BlockSpec((D, D), lambda i: (0, 0)),
            pl.BlockSpec((1, D), lambda i: (0, 0)),
            pl.BlockSpec((D, ROUTER_LANES), lambda i: (0, 0)),
        ],
        out_specs=[
            pl.BlockSpec((tm, D), lambda i: (i, 0)),
            pl.BlockSpec((tm, D), lambda i: (i, 0)),
            pl.BlockSpec((tm, ROUTER_LANES), lambda i: (i, 0)),
        ],
        out_shape=[
            jax.ShapeDtypeStruct((T, D), F32),
            jax.ShapeDtypeStruct((T, D), BF16),
            jax.ShapeDtypeStruct((T, ROUTER_LANES), F32),
        ],
        compiler_params=_cparams("parallel"),
        name="out_proj",
    )(x2d, merged, w_out, norm_g, wr_cat)


def _router_split(w_router):
    hi = w_router.astype(BF16)
    lo = (w_router - hi.astype(F32)).astype(BF16)
    pad = jnp.zeros((w_router.shape[0], ROUTER_LANES - 2 * N_EXPERTS), BF16)
    return jnp.concatenate([hi, lo, pad], axis=1)


def _expert_kernel(xs_ref, wg_ref, wu_ref, wd_ref, gs_ref, o_ref):
    f = pl.program_id(2)

    @pl.when(f == 0)
    def _():
        o_ref[0] = jnp.zeros(o_ref.shape[1:], F32)

    xs = xs_ref[0]
    a = jnp.dot(xs, wg_ref[0], preferred_element_type=F32)
    u = jnp.dot(xs, wu_ref[0], preferred_element_type=F32)
    hmid = (jax.nn.silu(a) * u).astype(BF16)
    o_ref[0] += jnp.dot(hmid, wd_ref[0], preferred_element_type=F32)

    @pl.when(f == pl.num_programs(2) - 1)
    def _():
        o_ref[0] *= gs_ref[0]


def _experts(xs, w_gate, w_up, w_down, gscale, tm=1024, tf=512):
    E, C, D = xs.shape
    F = w_gate.shape[2]
    tm, tf = min(tm, C), min(tf, F)
    return pl.pallas_call(
        _expert_kernel,
        grid=(E, C // tm, F // tf),
        in_specs=[
            pl.BlockSpec((1, tm, D), lambda e, m, f: (e, m, 0)),
            pl.BlockSpec((1, D, tf), lambda e, m, f: (e, 0, f)),
            pl.BlockSpec((1, D, tf), lambda e, m, f: (e, 0, f)),
            pl.BlockSpec((1, tf, D), lambda e, m, f: (e, f, 0)),
            pl.BlockSpec((1, tm, 1), lambda e, m, f: (e, m, 0)),
        ],
        out_specs=pl.BlockSpec((1, tm, D), lambda e, m, f: (e, m, 0)),
        out_shape=jax.ShapeDtypeStruct((E, C, D), F32),
        compiler_params=_cparams("parallel", "parallel", "arbitrary"),
        name="expert_swiglu",
    )(xs, w_gate, w_up, w_down, gscale)


COMBINE_TM = 256
COMBINE_W = 128


def _combine_kernel(tile_ref, first_ref, x1_ref, tok_ref, rows_ref, o_ref):
    g = pl.program_id(0)
    tm, w = o_ref.shape[0], rows_ref.shape[0]

    @pl.when(first_ref[g] == 1)
    def _():
        o_ref[...] = x1_ref[...]

    token = tile_ref[g] * tm + lax.broadcasted_iota(jnp.int32, (tm, w), 0)
    onehot = jnp.where(token == tok_ref[0], 1.0, 0.0).astype(BF16)
    rows = rows_ref[...]
    hi = rows.astype(BF16)
    lo = (rows - hi.astype(F32)).astype(BF16)
    o_ref[...] += (jnp.dot(onehot, hi, preferred_element_type=F32)
                   + jnp.dot(onehot, lo, preferred_element_type=F32))


def _combine_plan(idx, num_tokens, tm, w):
    n = idx.size
    n_tiles = num_tokens // tm
    n_win = n // w + n_tiles
    flat_tok = idx.reshape(-1).astype(jnp.int32)
    order = jnp.argsort(flat_tok).astype(jnp.int32)
    tok_sorted = flat_tok[order]
    starts = jnp.searchsorted(tok_sorted, jnp.arange(n_tiles + 1, dtype=jnp.int32) * tm).astype(jnp.int32)
    wins = jnp.maximum(1, (starts[1:] - starts[:-1] + w - 1) // w)
    win_start = jnp.concatenate([jnp.zeros((1,), jnp.int32), jnp.cumsum(wins).astype(jnp.int32)])
    g = jnp.arange(n_win, dtype=jnp.int32)
    win_tile = jnp.clip(jnp.searchsorted(win_start, g, side="right").astype(jnp.int32) - 1, 0, n_tiles - 1)
    first = (g == win_start[win_tile]).astype(jnp.int32)
    slot = jnp.arange(n_win * w, dtype=jnp.int32)
    tile = win_tile[slot // w]
    k = starts[tile] + slot - win_start[tile] * w
    valid = k < starts[tile + 1]
    k = jnp.minimum(k, n - 1)
    src = order[k]
    tok = jnp.where(valid, tok_sorted[k], -1)
    return win_tile, first, src, tok.reshape(n_win, 1, w)


def _combine(x1, ye_flat, idx):
    T, D = x1.shape
    tm, w = min(COMBINE_TM, T), COMBINE_W
    win_tile, first, src, tok = _combine_plan(idx, T, tm, w)
    rows = ye_flat[src]
    return pl.pallas_call(
        _combine_kernel,
        grid_spec=pltpu.PrefetchScalarGridSpec(
            num_scalar_prefetch=2,
            grid=(win_tile.shape[0],),
            in_specs=[
                pl.BlockSpec((tm, D), lambda g, tile, first: (tile[g], 0)),
                pl.BlockSpec((1, 1, w), lambda g, tile, first: (g, 0, 0)),
                pl.BlockSpec((w, D), lambda g, tile, first: (g, 0)),
            ],
            out_specs=pl.BlockSpec((tm, D), lambda g, tile, first: (tile[g], 0)),
        ),
        out_shape=jax.ShapeDtypeStruct((T, D), F32),
        compiler_params=_cparams("arbitrary"),
        name="expert_combine",
    )(win_tile, first, x1, tok, rows)


def _layer(x, lw, tabs, bias_tables):
    B, S, D = x.shape
    T = B * S
    tm = min(512, S)
    x2d = x.reshape(T, D)
    qkv = _qkv_proj(x2d, lw["norm_attn"], lw["w_qkv"], lw["qkn"], tabs, S, tm).reshape(B, S, QKV_W)
    oa = _attn_a(qkv, lw["sink_a"]).reshape(T, A_Q)
    ob = _attn_b(qkv).reshape(T, B_Q)
    oc = _attn_c(qkv, bias_tables).reshape(T, C_W)
    merged = _merge(x2d, lw["norm_attn"], oa, ob, oc, lw["w_gates"], lw["wba"], lw["wbb"], lw["wbc"], tm)
    x1, h2, logits = _outproj(x2d, merged, lw["w_out"], lw["norm_ffn"], lw["wr_cat"], min(256, S))

    cap = CAPACITY_FACTOR * T // N_EXPERTS
    aff = jax.nn.softmax(logits[:, :N_EXPERTS], axis=-1)
    gsel, idx = lax.top_k(aff.T, cap)
    xs = h2[idx]
    ye = _experts(xs, lw["w_gate"], lw["w_up"], lw["w_down"], gsel[..., None])
    return _combine(x1, ye.reshape(-1, D), idx).reshape(B, S, D)


def kernel(x_prompt, x_sample, norm_attn, w_in, qk_norm, sink_a, rpb_c, w_branch_a, w_branch_b, w_branch_c,
           w_out, norm_ffn, w_router, w_gate, w_up, w_down):
    depth = w_in.shape[0]
    layers = []
    for l in range(depth):
        layers.append(dict(
            norm_attn=norm_attn[l][None, :],
            w_qkv=_col_tiles(w_in[l, :, :QKV_W].astype(BF16), MATMUL_TN),
            w_gates=_col_tiles(w_in[l, :, QKV_W:].astype(BF16), MATMUL_TN),
            qkn=jnp.pad(qk_norm[l], ((0, 2), (0, 0))),
            sink_a=sink_a[l],
            rpb=rpb_c[l],
            wba=_col_tiles(w_branch_a[l].astype(BF16), MATMUL_TN),
            wbb=_col_tiles(w_branch_b[l].astype(BF16), MATMUL_TN),
            wbc=_col_tiles(w_branch_c[l].astype(BF16), MATMUL_TN),
            w_out=w_out[l].astype(BF16),
            norm_ffn=norm_ffn[l][None, :],
            wr_cat=_router_split(w_router[l]),
            w_gate=w_gate[l].astype(BF16),
            w_up=w_up[l].astype(BF16),
            w_down=w_down[l].astype(BF16),
        ))

    def trunk(x):
        S = x.shape[1]
        tabs = _rope_tables(S)
        for lw in layers:
            x = _layer(x, lw, tabs, _nbr_bias_tables(lw["rpb"], S // GRID_W))
        return x

    return trunk(x_prompt), trunk(x_sample)
```

```python
import functools
import math

import jax
import jax.numpy as jnp
from jax import lax
from jax.experimental import pallas as pl
from jax.experimental.pallas import tpu as pltpu

HEAD_DIM = 128
GRID_W = 64
ROPE_THETA = 10000.0
EPS = 1e-6
A_HEADS = 8
A_KV_HEADS = 2
WINDOW = 128
B_HEADS = 8
B_KV_HEADS = 2
C_HEADS = 8
NB_ROWS = 8
NB_COLS = 16
N_EXPERTS = 16
CAPACITY_FACTOR = 2
NEG_INF = -1e30

A_Q = A_HEADS * HEAD_DIM
A_KV = A_KV_HEADS * HEAD_DIM
B_Q = B_HEADS * HEAD_DIM
B_KV = B_KV_HEADS * HEAD_DIM
C_W = C_HEADS * HEAD_DIM
QKV_W = A_Q + 2 * A_KV + B_Q + 2 * B_KV + 3 * C_W
QA_H, QB_H, QC_H, KC_H, KA_H, KB_H, VA_H, VB_H, VC_H = 0, 8, 16, 24, 32, 34, 36, 38, 40
W_IN_SEGMENTS = ((0, A_Q), (A_Q + 2 * A_KV, B_Q), (A_Q + 2 * A_KV + B_Q + 2 * B_KV, C_W),
                 (A_Q + 2 * A_KV + B_Q + 2 * B_KV + C_W, C_W), (A_Q, A_KV), (A_Q + 2 * A_KV + B_Q, B_KV),
                 (A_Q + A_KV, A_KV), (A_Q + 2 * A_KV + B_Q + B_KV, B_KV),
                 (A_Q + 2 * A_KV + B_Q + 2 * B_KV + 2 * C_W, C_W))
QK_SCALE = 1.0 / math.sqrt(HEAD_DIM)
LOG2E = math.log2(math.e)

VMEM_LIMIT_BYTES = 52 * 1024 * 1024
ROUTER_LANES = 128
MATMUL_TN = 512
QKV_TN = 1024
QKV_SUB_HEADS = 2

BF16 = jnp.bfloat16
F32 = jnp.float32


def _cparams(*sem):
    return pltpu.CompilerParams(dimension_semantics=sem, vmem_limit_bytes=VMEM_LIMIT_BYTES)


def _rms_rows(x, g):
    ms = jnp.mean(x * x, axis=-1, keepdims=True)
    return x * lax.rsqrt(ms + EPS) * g


def _qkv_kernel(x_ref, g_ref, w_ref, qkn_ref, c1_ref, s1_ref, ca_ref, sa_ref, sb_ref, o_ref, h_ref, *,
                heads_per_tile):
    j = pl.program_id(1)

    @pl.when(j == 0)
    def _():
        h_ref[...] = _rms_rows(x_ref[...], g_ref[...]).astype(BF16)

    def tile(fns):
        for c in range(0, len(fns), QKV_SUB_HEADS):
            cols = slice(c * HEAD_DIM, (c + QKV_SUB_HEADS) * HEAD_DIM)
            acc = jnp.dot(h_ref[...], w_ref[0, :, cols], preferred_element_type=F32)
            for u in range(QKV_SUB_HEADS):
                sl = slice((c + u) * HEAD_DIM, (c + u + 1) * HEAD_DIM)
                o_ref[:, sl] = fns[c + u](acc[:, u * HEAD_DIM:(u + 1) * HEAD_DIM]).astype(o_ref.dtype)

    def norm(a, gi, scale):
        return _rms_rows(a, qkn_ref[gi:gi + 1, :] * scale)

    def rope1d(y):
        return y * c1_ref[...] + pltpu.roll(y, 64, 1) * s1_ref[...]

    def axial(y):
        return y * ca_ref[...] + pltpu.roll(y, 96, 1) * sa_ref[...] + pltpu.roll(y, 32, 1) * sb_ref[...]

    plain = lambda a: a
    segs = [
        (QA_H, QB_H, lambda a: rope1d(norm(a, 0, QK_SCALE))),
        (QB_H, QC_H, lambda a: axial(norm(a, 2, QK_SCALE * LOG2E))),
        (QC_H, KC_H, lambda a: norm(a, 4, QK_SCALE)),
        (KC_H, KA_H, lambda a: norm(a, 5, 1.0)),
        (KA_H, KB_H, lambda a: rope1d(norm(a, 1, 1.0))),
        (KB_H, VA_H, lambda a: axial(norm(a, 3, 1.0))),
        (VA_H, VC_H + C_HEADS, plain),
    ]
    head_fn = [fn for lo, hi, fn in segs for _ in range(hi - lo)]
    hpt = heads_per_tile
    tile_fns = [tuple(head_fn[t * hpt:(t + 1) * hpt]) for t in range(len(head_fn) // hpt)]
    lo = 0
    for t in range(1, len(tile_fns) + 1):
        if t == len(tile_fns) or tile_fns[t] != tile_fns[lo]:
            pl.when((j >= lo) & (j < t))(functools.partial(tile, tile_fns[lo]))
            lo = t


def _col_tiles(w, tn):
    K, N = w.shape
    return w.reshape(K, N // tn, tn).transpose(1, 0, 2)


PREP_COLS = 256


def _prep_cols_kernel(src_ref, w_ref, o_ref):
    del src_ref
    o_ref[0] = w_ref[0].astype(o_ref.dtype)


def _prep_col_tiles(w, layer, col_blocks, tn):
    K = w.shape[1]
    per = tn // PREP_COLS
    n = len(col_blocks)
    return pl.pallas_call(
        _prep_cols_kernel,
        grid_spec=pltpu.PrefetchScalarGridSpec(
            num_scalar_prefetch=1,
            grid=(n,),
            in_specs=[pl.BlockSpec((1, K, PREP_COLS), lambda j, src: (layer, 0, src[j]))],
            out_specs=pl.BlockSpec((1, K, PREP_COLS), lambda j, src: (j // per, 0, j % per)),
        ),
        out_shape=jax.ShapeDtypeStruct((n // per, K, tn), BF16),
        compiler_params=_cparams("arbitrary"),
        name="weight_prep",
    )(jnp.asarray(col_blocks, jnp.int32), w)


def _qkv_proj(x2d, norm_g, w_qkv, qkn, tabs, seq_len, tm):
    T, D = x2d.shape
    tn = w_qkv.shape[2]
    nseq = seq_len // tm
    tab_spec = pl.BlockSpec((tm, HEAD_DIM), lambda i, j: (i % nseq, 0))
    return pl.pallas_call(
        functools.partial(_qkv_kernel, heads_per_tile=tn // HEAD_DIM),
        grid=(T // tm, QKV_W // tn),
        in_specs=[
            pl.BlockSpec((tm, D), lambda i, j: (i, 0)),
            pl.BlockSpec((1, D), lambda i, j: (0, 0)),
            pl.BlockSpec((1, D, tn), lambda i, j: (j, 0, 0)),
            pl.BlockSpec((8, HEAD_DIM), lambda i, j: (0, 0)),
        ] + [tab_spec] * 5,
        out_specs=pl.BlockSpec((tm, tn), lambda i, j: (i, j)),
        out_shape=jax.ShapeDtypeStruct((T, QKV_W), BF16),
        scratch_shapes=[pltpu.VMEM((tm, D), BF16)],
        compiler_params=_cparams("parallel", "arbitrary"),
        name="qkv_proj",
    )(x2d, norm_g, w_qkv, qkn, *tabs)


def _rope_tables(seq_len):
    def cos_sin(pos, dim):
        inv = ROPE_THETA ** (-jnp.arange(0, dim, 2, dtype=F32) / dim)
        ang = pos.astype(F32)[:, None] * inv[None, :]
        return jnp.cos(ang), jnp.sin(ang)

    t = jnp.arange(seq_len, dtype=jnp.int32)
    c, s = cos_sin(t, HEAD_DIM)
    cr, sr = cos_sin(t // GRID_W, HEAD_DIM // 2)
    cc, sc = cos_sin(t % GRID_W, HEAD_DIM // 2)
    z = jnp.zeros_like(sr)
    c1 = jnp.concatenate([c, c], axis=-1)
    s1 = jnp.concatenate([-s, s], axis=-1)
    ca = jnp.concatenate([cr, cr, cc, cc], axis=-1)
    sa = jnp.concatenate([-sr, z, -sc, z], axis=-1)
    sb = jnp.concatenate([z, sr, z, sc], axis=-1)
    return c1, s1, ca, sa, sb


def _attn_a_kernel(sink_ref, q_ref, k_ref, v_ref, o_ref, *, tq, tk, seq_len):
    g = pl.program_id(1)
    qi = pl.program_id(2)
    q0 = qi * tq
    start = jnp.clip(q0 - WINDOW, 0, seq_len - tk)
    start = pl.multiple_of(start, WINDOW)
    k = k_ref[0, pl.ds(start, tk), :]
    v = v_ref[0, pl.ds(start, tk), :]
    qpos = q0 + lax.broadcasted_iota(jnp.int32, (tq, tk), 0)
    kpos = start + lax.broadcasted_iota(jnp.int32, (tq, tk), 1)
    valid = jnp.abs(qpos - kpos) <= WINDOW
    group = A_HEADS // A_KV_HEADS
    for h in range(group):
        sl = slice(h * HEAD_DIM, (h + 1) * HEAD_DIM)
        s = lax.dot_general(q_ref[0, :, sl], k, (((1,), (1,)), ((), ())), preferred_element_type=F32)
        s = jnp.where(valid, s, NEG_INF)
        sink = sink_ref[g * group + h]
        m = jnp.maximum(jnp.max(s, axis=-1, keepdims=True), sink)
        p = jnp.exp(s - m)
        denom = jnp.sum(p, axis=-1, keepdims=True) + jnp.exp(sink - m)
        o = jnp.dot(p.astype(BF16), v, preferred_element_type=F32)
        o_ref[0, :, sl] = (o / denom).astype(o_ref.dtype)


def _attn_a(qkv, sink, tq=256):
    B, S, _ = qkv.shape
    tk = tq + 2 * WINDOW
    gw = (A_HEADS // A_KV_HEADS) * HEAD_DIM
    return pl.pallas_call(
        functools.partial(_attn_a_kernel, tq=tq, tk=tk, seq_len=S),
        grid_spec=pltpu.PrefetchScalarGridSpec(
            num_scalar_prefetch=1,
            grid=(B, A_KV_HEADS, S // tq),
            in_specs=[
                pl.BlockSpec((1, tq, gw), lambda b, g, i, s: (b, i, QA_H * HEAD_DIM // gw + g)),
                pl.BlockSpec((1, S, HEAD_DIM), lambda b, g, i, s: (b, 0, KA_H + g)),
                pl.BlockSpec((1, S, HEAD_DIM), lambda b, g, i, s: (b, 0, VA_H + g)),
            ],
            out_specs=pl.BlockSpec((1, tq, gw), lambda b, g, i, s: (b, i, g)),
        ),
        out_shape=jax.ShapeDtypeStruct((B, S, A_Q), BF16),
        compiler_params=_cparams("parallel", "parallel", "arbitrary"),
        name="attn_window",
    )(sink, qkv, qkv, qkv)


def _attn_b_kernel(q_ref, k_ref, vt_ref, o_ref, m_ref, l_ref, acc_ref):
    ki = pl.program_id(3)

    @pl.when(ki == 0)
    def _():
        m_ref[...] = jnp.full(m_ref.shape, -jnp.inf, F32)
        l_ref[...] = jnp.zeros(l_ref.shape, F32)
        acc_ref[...] = jnp.zeros(acc_ref.shape, F32)

    k = k_ref[0]
    vt = vt_ref[0]
    group = B_HEADS // B_KV_HEADS
    for h in range(group):
        sl = slice(h * HEAD_DIM, (h + 1) * HEAD_DIM)
        st = lax.dot_general(k, q_ref[0, :, sl], (((1,), (1,)), ((), ())), preferred_element_type=F32)
        m_prev = m_ref[h]
        m_cur = jnp.maximum(m_prev, jnp.max(st, axis=0, keepdims=True))
        alpha = jnp.exp2(m_prev - m_cur)
        pt = jnp.exp2(st - m_cur)
        l_ref[h] = alpha * l_ref[h] + jnp.sum(pt, axis=0, keepdims=True)
        acc_ref[h] = alpha * acc_ref[h] + jnp.dot(vt, pt.astype(BF16), preferred_element_type=F32)
        m_ref[h] = m_cur

    @pl.when(ki == pl.num_programs(3) - 1)
    def _():
        for h in range(group):
            sl = slice(h * HEAD_DIM, (h + 1) * HEAD_DIM)
            o_ref[0, :, sl] = (acc_ref[h] / l_ref[h]).T.astype(o_ref.dtype)


def _attn_b(qkv, tq=512, tk=4096):
    B, S, _ = qkv.shape
    tq, tk = min(tq, S), min(tk, S)
    group = B_HEADS // B_KV_HEADS
    gw = group * HEAD_DIM
    vt = jnp.swapaxes(qkv[:, :, VB_H * HEAD_DIM:(VB_H + B_KV_HEADS) * HEAD_DIM], 1, 2)
    return pl.pallas_call(
        _attn_b_kernel,
        grid=(B, B_KV_HEADS, S // tq, S // tk),
        in_specs=[
            pl.BlockSpec((1, tq, gw), lambda b, g, i, k: (b, i, QB_H * HEAD_DIM // gw + g)),
            pl.BlockSpec((1, tk, HEAD_DIM), lambda b, g, i, k: (b, k, KB_H + g)),
            pl.BlockSpec((1, HEAD_DIM, tk), lambda b, g, i, k: (b, g, k)),
        ],
        out_specs=pl.BlockSpec((1, tq, gw), lambda b, g, i, k: (b, i, g)),
        out_shape=jax.ShapeDtypeStruct((B, S, B_Q), BF16),
        scratch_shapes=[
            pltpu.VMEM((group, 1, tq), F32),
            pltpu.VMEM((group, 1, tq), F32),
            pltpu.VMEM((group, HEAD_DIM, tq), F32),
        ],
        compiler_params=_cparams("parallel", "parallel", "parallel", "arbitrary"),
        name="attn_dense",
    )(qkv, qkv, vt)


C_QROWS = 8
C_KROWS = 16
C_TQ = C_QROWS * GRID_W
C_TK = C_KROWS * GRID_W


def _attn_c_kernel(q_ref, k_ref, vt_ref, bias_ref, o_ref, *, rows):
    i = pl.program_id(2)
    start_row = jnp.clip(i * C_QROWS - NB_ROWS // 2, 0, rows - C_KROWS)
    start = pl.multiple_of(start_row * GRID_W, (NB_ROWS // 2) * GRID_W)
    k = k_ref[0, pl.ds(start, C_TK), :]
    vt = vt_ref[0, :, pl.ds(start, C_TK)]
    st = lax.dot_general(k, q_ref[0], (((1,), (1,)), ((), ())), preferred_element_type=F32)
    st = st + bias_ref[0, 0]
    pt = jnp.exp(st - jnp.max(st, axis=0, keepdims=True))
    denom = jnp.sum(pt, axis=0, keepdims=True)
    ot = jnp.dot(vt, pt.astype(BF16), preferred_element_type=F32)
    o_ref[0] = (ot / denom).T.astype(o_ref.dtype)


def _nbr_bias_tables(rpb, rows):
    assert rows % C_QROWS == 0 and rows >= C_KROWS
    rpb = rpb.astype(F32)
    c = jnp.arange(GRID_W)[:, None]
    kc = jnp.arange(GRID_W)[None, :]
    cs = jnp.clip(c - NB_COLS // 2, 0, GRID_W - NB_COLS)
    col_valid = (kc >= cs) & (kc < cs + NB_COLS)
    col_hot = (kc - c + (NB_COLS - 1))[..., None] == jnp.arange(2 * NB_COLS - 1)
    col_hot = (col_hot & col_valid[..., None]).astype(F32)
    by_col = jnp.einsum("hrc,qkc->hrqk", rpb, col_hot, precision=lax.Precision.HIGHEST)
    tables = []
    for first_row in (0, min(C_QROWS, rows - C_KROWS + NB_ROWS // 2), rows - C_QROWS):
        start = min(max(first_row - NB_ROWS // 2, 0), rows - C_KROWS)
        r = first_row + jnp.arange(C_QROWS)[:, None]
        kr = start + jnp.arange(C_KROWS)[None, :]
        rs = jnp.clip(r - NB_ROWS // 2, 0, rows - NB_ROWS)
        row_valid = (kr >= rs) & (kr < rs + NB_ROWS)
        row_hot = (kr - r + (NB_ROWS - 1))[..., None] == jnp.arange(2 * NB_ROWS - 1)
        row_hot = (row_hot & row_valid[..., None]).astype(F32)
        bias = jnp.einsum("djr,hrqk->hjkdq", row_hot, by_col, precision=lax.Precision.HIGHEST)
        valid = row_valid.T[:, None, :, None] & col_valid.T[None, :, None, :]
        bias = jnp.where(valid[None], bias, NEG_INF)
        tables.append(bias.reshape(C_HEADS, C_TK, C_TQ))
    return jnp.stack(tables)


def _attn_c(qkv, bias_tables):
    B, S, _ = qkv.shape
    rows = S // GRID_W
    n = S // C_TQ
    vt = jnp.swapaxes(qkv[:, :, VC_H * HEAD_DIM:(VC_H + C_HEADS) * HEAD_DIM], 1, 2)

    def bias_idx(h, b, i):
        return (jnp.where(i == 0, 0, jnp.where(i == n - 1, 2, 1)), h, 0, 0)

    return pl.pallas_call(
        functools.partial(_attn_c_kernel, rows=rows),
        grid=(C_HEADS, B, n),
        in_specs=[
            pl.BlockSpec((1, C_TQ, HEAD_DIM), lambda h, b, i: (b, i, QC_H + h)),
            pl.BlockSpec((1, S, HEAD_DIM), lambda h, b, i: (b, 0, KC_H + h)),
            pl.BlockSpec((1, HEAD_DIM, S), lambda h, b, i: (b, h, 0)),
            pl.BlockSpec((1, 1, C_TK, C_TQ), bias_idx),
        ],
        out_specs=pl.BlockSpec((1, C_TQ, HEAD_DIM), lambda h, b, i: (b, i, h)),
        out_shape=jax.ShapeDtypeStruct((B, S, C_W), BF16),
        compiler_params=_cparams("parallel", "parallel", "arbitrary"),
        name="attn_nbr",
    )(qkv, qkv, vt, bias_tables)


def _merge_kernel(x_ref, g_ref, oa_ref, ob_ref, oc_ref, wga_ref, wgb_ref, wgc_ref, wa_ref, wb_ref, wc_ref,
                  o_ref, h_ref):
    @pl.when(pl.program_id(1) == 0)
    def _():
        h_ref[...] = _rms_rows(x_ref[...], g_ref[...]).astype(BF16)

    h = h_ref[...]
    merged = None
    for o_r, wg_r, w_r in ((oa_ref, wga_ref, wa_ref), (ob_ref, wgb_ref, wb_ref), (oc_ref, wgc_ref, wc_ref)):
        gate = jnp.dot(h, wg_r[0], preferred_element_type=F32)
        y = jnp.dot(o_r[...], w_r[0], preferred_element_type=F32)
        term = jax.nn.sigmoid(gate) * y
        merged = term if merged is None else merged + term
    o_ref[...] = merged.astype(o_ref.dtype)


def _merge(x2d, norm_g, oa, ob, oc, w_gates, wba, wbb, wbc, tm):
    T, D = x2d.shape
    tn = w_gates.shape[2]
    nj = D // tn
    o_spec = lambda w: pl.BlockSpec((tm, w), lambda i, j: (i, 0))
    wg_spec = lambda k: pl.BlockSpec((1, D, tn), lambda i, j: (k * nj + j, 0, 0))
    wb_spec = lambda w: pl.BlockSpec((1, w, tn), lambda i, j: (j, 0, 0))
    return pl.pallas_call(
        _merge_kernel,
        grid=(T // tm, nj),
        in_specs=[
            pl.BlockSpec((tm, D), lambda i, j: (i, 0)),
            pl.BlockSpec((1, D), lambda i, j: (0, 0)),
            o_spec(A_Q), o_spec(B_Q), o_spec(C_W),
            wg_spec(0), wg_spec(1), wg_spec(2),
            wb_spec(A_Q), wb_spec(B_Q), wb_spec(C_W),
        ],
        out_specs=pl.BlockSpec((tm, tn), lambda i, j: (i, j)),
        out_shape=jax.ShapeDtypeStruct((T, D), BF16),
        scratch_shapes=[pltpu.VMEM((tm, D), BF16)],
        compiler_params=_cparams("parallel", "arbitrary"),
        name="gated_merge",
    )(x2d, norm_g, oa, ob, oc, w_gates, w_gates, w_gates, wba, wbb, wbc)


def _outproj_kernel(x_ref, m_ref, w_ref, g_ref, wr_ref, x1_ref, h2_ref, lg_ref):
    x1 = x_ref[...] + jnp.dot(m_ref[...], w_ref[...], preferred_element_type=F32)
    x1_ref[...] = x1
    h2 = _rms_rows(x1, g_ref[...])
    hi = h2.astype(BF16)
    lo = (h2 - hi.astype(F32)).astype(BF16)
    h2_ref[...] = hi
    wr = wr_ref[...]
    r = jnp.dot(hi, wr, preferred_element_type=F32) + jnp.dot(lo, wr, preferred_element_type=F32)
    lg_ref[...] = r + pltpu.roll(r, ROUTER_LANES - N_EXPERTS, 1)


def _outproj(x2d, merged, w_out, norm_g, wr_cat, tm):
    T, D = x2d.shape
    return pl.pallas_call(
        _outproj_kernel,
        grid=(T // tm,),
        in_specs=[
            pl.BlockSpec((tm, D), lambda i: (i, 0)),
            pl.BlockSpec((tm, D), lambda i: (i, 0)),
            pl.BlockSpec((D, D), lambda i: (0, 0)),
            pl.BlockSpec((1, D), lambda i: (0, 0)),
            pl.BlockSpec((D, ROUTER_LANES), lambda i: (0, 0)),
        ],
        out_specs=[
            pl.BlockSpec((tm, D), lambda i: (i, 0)),
            pl.BlockSpec((tm, D), lambda i: (i, 0)),
            pl.BlockSpec((tm, ROUTER_LANES), lambda i: (i, 0)),
        ],
        out_shape=[
            jax.ShapeDtypeStruct((T, D), F32),
            jax.ShapeDtypeStruct((T, D), BF16),
            jax.ShapeDtypeStruct((T, ROUTER_LANES), F32),
        ],
        compiler_params=_cparams("parallel"),
        name="out_proj",
    )(x2d, merged, w_out, norm_g, wr_cat)


def _router_split(w_router):
    hi = w_router.astype(BF16)
    lo = (w_router - hi.astype(F32)).astype(BF16)
    pad = jnp.zeros((w_router.shape[0], ROUTER_LANES - 2 * N_EXPERTS), BF16)
    return jnp.concatenate([hi, lo, pad], axis=1)


def _expert_kernel(xs_ref, wg_ref, wu_ref, wd_ref, gs_ref, o_ref):
    f = pl.program_id(2)

    @pl.when(f == 0)
    def _():
        o_ref[0] = jnp.zeros(o_ref.shape[1:], F32)

    xs = xs_ref[0]
    a = jnp.dot(xs, wg_ref[0], preferred_element_type=F32)
    u = jnp.dot(xs, wu_ref[0], preferred_element_type=F32)
    hmid = (jax.nn.silu(a) * u).astype(BF16)
    o_ref[0] += jnp.dot(hmid, wd_ref[0], preferred_element_type=F32)

    @pl.when(f == pl.num_programs(2) - 1)
    def _():
        o_ref[0] *= gs_ref[0]


def _experts(xs, w_gate, w_up, w_down, gscale, tm=1024, tf=512):
    E, C, D = xs.shape
    F = w_gate.shape[2]
    tm, tf = min(tm, C), min(tf, F)
    return pl.pallas_call(
        _expert_kernel,
        grid=(E, C // tm, F // tf),
        in_specs=[
            pl.BlockSpec((1, tm, D), lambda e, m, f: (e, m, 0)),
            pl.BlockSpec((1, D, tf), lambda e, m, f: (e, 0, f)),
            pl.BlockSpec((1, D, tf), lambda e, m, f: (e, 0, f)),
            pl.BlockSpec((1, tf, D), lambda e, m, f: (e, f, 0)),
            pl.BlockSpec((1, tm, 1), lambda e, m, f: (e, m, 0)),
        ],
        out_specs=pl.BlockSpec((1, tm, D), lambda e, m, f: (e, m, 0)),
        out_shape=jax.ShapeDtypeStruct((E, C, D), F32),
        compiler_params=_cparams("parallel", "parallel", "arbitrary"),
        name="expert_swiglu",
    )(xs, w_gate, w_up, w_down, gscale)


COMBINE_TM = 256
COMBINE_W = 256


def _combine_kernel(tile_ref, win_ref, first_ref, x1_ref, tok_ref, rows_ref, o_ref):
    g = pl.program_id(0)
    tm, w = o_ref.shape[0], rows_ref.shape[0]

    @pl.when(first_ref[g] == 1)
    def _():
        o_ref[...] = x1_ref[...]

    token = tile_ref[g] * tm + lax.broadcasted_iota(jnp.int32, (tm, w), 0)
    onehot = jnp.where(token == tok_ref[0], 1.0, 0.0).astype(BF16)
    rows = rows_ref[...]
    hi = rows.astype(BF16)
    lo = (rows - hi.astype(F32)).astype(BF16)
    o_ref[...] += (jnp.dot(onehot, hi, preferred_element_type=F32)
                   + jnp.dot(onehot, lo, preferred_element_type=F32))


def _combine_plan(idx, num_tokens, tm, w):
    n = idx.size
    n_tiles = num_tokens // tm
    n_win = n // w
    n_steps = n_win + n_tiles
    flat_tok = idx.reshape(-1).astype(jnp.int32)
    tok_sorted, order = lax.sort_key_val(flat_tok, jnp.arange(n, dtype=jnp.int32))
    bounds = jnp.arange(n_tiles + 1, dtype=jnp.int32) * tm
    starts = jnp.sum((tok_sorted[None, :] < bounds[:, None]).astype(jnp.int32), axis=1)
    first_win = jnp.minimum(starts[:-1] // w, n_win - 1)
    last_win = jnp.clip((starts[1:] - 1) // w, first_win, n_win - 1)
    step_start = jnp.concatenate([jnp.zeros((1,), jnp.int32), jnp.cumsum(last_win - first_win + 1)])
    g = jnp.arange(n_steps, dtype=jnp.int32)
    tile = jnp.minimum(jnp.sum((step_start[None, 1:] <= g[:, None]).astype(jnp.int32), axis=1), n_tiles - 1)
    of_tile = tile[:, None] == jnp.arange(n_tiles, dtype=jnp.int32)[None, :]
    tile_first_win = jnp.sum(jnp.where(of_tile, first_win[None, :], 0), axis=1)
    tile_step_start = jnp.sum(jnp.where(of_tile, step_start[None, :-1], 0), axis=1)
    active = g < step_start[n_tiles]
    win = jnp.where(active, tile_first_win + g - tile_step_start, n_win)
    first = (active & (g == tile_step_start)).astype(jnp.int32)
    order = jnp.concatenate([order, jnp.zeros((w,), jnp.int32)])
    tok = jnp.concatenate([tok_sorted, jnp.full((w,), -1, jnp.int32)]).reshape(n_win + 1, 1, w)
    return order, tok, tile, win, first


def _combine(x1, ye_flat, idx):
    T, D = x1.shape
    tm, w = min(COMBINE_TM, T), COMBINE_W
    order, tok, tile, win, first = _combine_plan(idx, T, tm, w)
    rows = ye_flat[order]
    return pl.pallas_call(
        _combine_kernel,
        grid_spec=pltpu.PrefetchScalarGridSpec(
            num_scalar_prefetch=3,
            grid=(tile.shape[0],),
            in_specs=[
                pl.BlockSpec((tm, D), lambda g, tile, win, first: (tile[g], 0)),
                pl.BlockSpec((1, 1, w), lambda g, tile, win, first: (win[g], 0, 0)),
                pl.BlockSpec((w, D), lambda g, tile, win, first: (win[g], 0)),
            ],
            out_specs=pl.BlockSpec((tm, D), lambda g, tile, win, first: (tile[g], 0)),
        ),
        out_shape=jax.ShapeDtypeStruct((T, D), F32),
        compiler_params=_cparams("arbitrary"),
        name="expert_combine",
    )(tile, win, first, x1, tok, rows)


def _layer(x, lw, tabs, bias_tables):
    B, S, D = x.shape
    T = B * S
    tm = min(512, S)
    x2d = x.reshape(T, D)
    qkv = _qkv_proj(x2d, lw["norm_attn"], lw["w_qkv"], lw["qkn"], tabs, S, tm).reshape(B, S, QKV_W)
    oa = _attn_a(qkv, lw["sink_a"]).reshape(T, A_Q)
    ob = _attn_b(qkv).reshape(T, B_Q)
    oc = _attn_c(qkv, bias_tables).reshape(T, C_W)
    merged = _merge(x2d, lw["norm_attn"], oa, ob, oc, lw["w_gates"], lw["wba"], lw["wbb"], lw["wbc"], tm)
    x1, h2, logits = _outproj(x2d, merged, lw["w_out"], lw["norm_ffn"], lw["wr_cat"], min(256, S))

    cap = CAPACITY_FACTOR * T // N_EXPERTS
    aff = jax.nn.softmax(logits[:, :N_EXPERTS], axis=-1)
    gsel, idx = lax.top_k(aff.T, cap)
    xs = h2[idx]
    ye = _experts(xs, lw["w_gate"], lw["w_up"], lw["w_down"], gsel[..., None])
    return _combine(x1, ye.reshape(-1, D), idx).reshape(B, S, D)


def kernel(x_prompt, x_sample, norm_attn, w_in, qk_norm, sink_a, rpb_c, w_branch_a, w_branch_b, w_branch_c,
           w_out, norm_ffn, w_router, w_gate, w_up, w_down):
    depth = w_in.shape[0]
    qkv_blocks = [b for o, n in W_IN_SEGMENTS for b in range(o // PREP_COLS, (o + n) // PREP_COLS)]
    gate_blocks = list(range(QKV_W // PREP_COLS, w_in.shape[2] // PREP_COLS))
    layers = []
    for l in range(depth):
        layers.append(dict(
            norm_attn=norm_attn[l][None, :],
            w_qkv=_prep_col_tiles(w_in, l, qkv_blocks, QKV_TN),
            w_gates=_prep_col_tiles(w_in, l, gate_blocks, MATMUL_TN),
            qkn=jnp.pad(qk_norm[l], ((0, 2), (0, 0))),
            sink_a=sink_a[l],
            rpb=rpb_c[l],
            wba=_col_tiles(w_branch_a[l].astype(BF16), MATMUL_TN),
            wbb=_col_tiles(w_branch_b[l].astype(BF16), MATMUL_TN),
            wbc=_col_tiles(w_branch_c[l].astype(BF16), MATMUL_TN),
            w_out=w_out[l].astype(BF16),
            norm_ffn=norm_ffn[l][None, :],
            wr_cat=_router_split(w_router[l]),
            w_gate=w_gate[l].astype(BF16),
            w_up=w_up[l].astype(BF16),
            w_down=w_down[l].astype(BF16),
        ))

    def trunk(x):
        S = x.shape[1]
        tabs = _rope_tables(S)
        for lw in layers:
            x = _layer(x, lw, tabs, _nbr_bias_tables(lw["rpb"], S // GRID_W))
        return x

    return trunk(x_prompt), trunk(x_sample)
```

```python
import functools
import math

import jax
import jax.numpy as jnp
from jax import lax
from jax.experimental import pallas as pl
from jax.experimental.pallas import tpu as pltpu

HEAD_DIM = 128
GRID_W = 64
ROPE_THETA = 10000.0
EPS = 1e-6
A_HEADS = 8
A_KV_HEADS = 2
WINDOW = 128
B_HEADS = 8
B_KV_HEADS = 2
C_HEADS = 8
NB_ROWS = 8
NB_COLS = 16
N_EXPERTS = 16
CAPACITY_FACTOR = 2
NEG_INF = -1e30

A_Q = A_HEADS * HEAD_DIM
A_KV = A_KV_HEADS * HEAD_DIM
B_Q = B_HEADS * HEAD_DIM
B_KV = B_KV_HEADS * HEAD_DIM
C_W = C_HEADS * HEAD_DIM
QKV_W = A_Q + 2 * A_KV + B_Q + 2 * B_KV + 3 * C_W
QA_H, QB_H, QC_H, KC_H, KA_H, KB_H, VA_H, VB_H, VC_H = 0, 8, 16, 24, 32, 34, 36, 38, 40
W_IN_SEGMENTS = ((0, A_Q), (A_Q + 2 * A_KV, B_Q), (A_Q + 2 * A_KV + B_Q + 2 * B_KV, C_W),
                 (A_Q + 2 * A_KV + B_Q + 2 * B_KV + C_W, C_W), (A_Q, A_KV), (A_Q + 2 * A_KV + B_Q, B_KV),
                 (A_Q + A_KV, A_KV), (A_Q + 2 * A_KV + B_Q + B_KV, B_KV),
                 (A_Q + 2 * A_KV + B_Q + 2 * B_KV + 2 * C_W, C_W))
QK_SCALE = 1.0 / math.sqrt(HEAD_DIM)
LOG2E = math.log2(math.e)

VMEM_LIMIT_BYTES = 52 * 1024 * 1024
ROUTER_LANES = 128
MATMUL_TN = 512
QKV_TN = 1024
QKV_SUB_HEADS = 2

BF16 = jnp.bfloat16
F32 = jnp.float32


def _cparams(*sem):
    return pltpu.CompilerParams(dimension_semantics=sem, vmem_limit_bytes=VMEM_LIMIT_BYTES)


def _rms_rows(x, g):
    ms = jnp.mean(x * x, axis=-1, keepdims=True)
    return x * lax.rsqrt(ms + EPS) * g


def _qkv_kernel(x_ref, g_ref, w_ref, qkn_ref, c1_ref, s1_ref, ca_ref, sa_ref, sb_ref, o_ref, h_ref, *,
                heads_per_tile):
    j = pl.program_id(1)

    @pl.when(j == 0)
    def _():
        h_ref[...] = _rms_rows(x_ref[...], g_ref[...]).astype(BF16)

    def tile(fns):
        outs = []
        for c in range(0, len(fns), QKV_SUB_HEADS):
            cols = slice(c * HEAD_DIM, (c + QKV_SUB_HEADS) * HEAD_DIM)
            acc = jnp.dot(h_ref[...], w_ref[0, :, cols], preferred_element_type=F32)
            for u in range(QKV_SUB_HEADS):
                outs.append(fns[c + u](acc[:, u * HEAD_DIM:(u + 1) * HEAD_DIM]).astype(o_ref.dtype))
        o_ref[...] = jnp.concatenate(outs, axis=1)

    def norm(a, gi, scale):
        return _rms_rows(a, qkn_ref[gi:gi + 1, :] * scale)

    def rope1d(y):
        return y * c1_ref[...] + pltpu.roll(y, 64, 1) * s1_ref[...]

    def axial(y):
        return y * ca_ref[...] + pltpu.roll(y, 96, 1) * sa_ref[...] + pltpu.roll(y, 32, 1) * sb_ref[...]

    plain = lambda a: a
    segs = [
        (QA_H, QB_H, lambda a: rope1d(norm(a, 0, QK_SCALE))),
        (QB_H, QC_H, lambda a: axial(norm(a, 2, QK_SCALE * LOG2E))),
        (QC_H, KC_H, lambda a: norm(a, 4, QK_SCALE)),
        (KC_H, KA_H, lambda a: norm(a, 5, 1.0)),
        (KA_H, KB_H, lambda a: rope1d(norm(a, 1, 1.0))),
        (KB_H, VA_H, lambda a: axial(norm(a, 3, 1.0))),
        (VA_H, VC_H + C_HEADS, plain),
    ]
    head_fn = [fn for lo, hi, fn in segs for _ in range(hi - lo)]
    hpt = heads_per_tile
    tile_fns = [tuple(head_fn[t * hpt:(t + 1) * hpt]) for t in range(len(head_fn) // hpt)]
    lo = 0
    for t in range(1, len(tile_fns) + 1):
        if t == len(tile_fns) or tile_fns[t] != tile_fns[lo]:
            pl.when((j >= lo) & (j < t))(functools.partial(tile, tile_fns[lo]))
            lo = t


def _col_tiles(w, tn):
    K, N = w.shape
    return w.reshape(K, N // tn, tn).transpose(1, 0, 2)


PREP_COLS = 256


def _prep_cols_kernel(src_ref, w_ref, o_ref):
    del src_ref
    o_ref[0] = w_ref[0].astype(o_ref.dtype)


def _prep_col_tiles(w, layer, col_blocks, tn):
    K = w.shape[1]
    per = tn // PREP_COLS
    n = len(col_blocks)
    return pl.pallas_call(
        _prep_cols_kernel,
        grid_spec=pltpu.PrefetchScalarGridSpec(
            num_scalar_prefetch=1,
            grid=(n,),
            in_specs=[pl.BlockSpec((1, K, PREP_COLS), lambda j, src: (layer, 0, src[j]))],
            out_specs=pl.BlockSpec((1, K, PREP_COLS), lambda j, src: (j // per, 0, j % per)),
        ),
        out_shape=jax.ShapeDtypeStruct((n // per, K, tn), BF16),
        compiler_params=_cparams("arbitrary"),
        name="weight_prep",
    )(jnp.asarray(col_blocks, jnp.int32), w)


def _qkv_proj(x2d, norm_g, w_qkv, qkn, tabs, seq_len, tm):
    T, D = x2d.shape
    tn = w_qkv.shape[2]
    nseq = seq_len // tm
    tab_spec = pl.BlockSpec((tm, HEAD_DIM), lambda i, j: (i % nseq, 0))
    return pl.pallas_call(
        functools.partial(_qkv_kernel, heads_per_tile=tn // HEAD_DIM),
        grid=(T // tm, QKV_W // tn),
        in_specs=[
            pl.BlockSpec((tm, D), lambda i, j: (i, 0)),
            pl.BlockSpec((1, D), lambda i, j: (0, 0)),
            pl.BlockSpec((1, D, tn), lambda i, j: (j, 0, 0)),
            pl.BlockSpec((8, HEAD_DIM), lambda i, j: (0, 0)),
        ] + [tab_spec] * 5,
        out_specs=pl.BlockSpec((tm, tn), lambda i, j: (i, j)),
        out_shape=jax.ShapeDtypeStruct((T, QKV_W), BF16),
        scratch_shapes=[pltpu.VMEM((tm, D), BF16)],
        compiler_params=_cparams("parallel", "arbitrary"),
        name="qkv_proj",
    )(x2d, norm_g, w_qkv, qkn, *tabs)


def _rope_tables(seq_len):
    def cos_sin(pos, dim):
        inv = ROPE_THETA ** (-jnp.arange(0, dim, 2, dtype=F32) / dim)
        ang = pos.astype(F32)[:, None] * inv[None, :]
        return jnp.cos(ang), jnp.sin(ang)

    t = jnp.arange(seq_len, dtype=jnp.int32)
    c, s = cos_sin(t, HEAD_DIM)
    cr, sr = cos_sin(t // GRID_W, HEAD_DIM // 2)
    cc, sc = cos_sin(t % GRID_W, HEAD_DIM // 2)
    z = jnp.zeros_like(sr)
    c1 = jnp.concatenate([c, c], axis=-1)
    s1 = jnp.concatenate([-s, s], axis=-1)
    ca = jnp.concatenate([cr, cr, cc, cc], axis=-1)
    sa = jnp.concatenate([-sr, z, -sc, z], axis=-1)
    sb = jnp.concatenate([z, sr, z, sc], axis=-1)
    return c1, s1, ca, sa, sb


def _attn_a_kernel(sink_ref, q_ref, k_ref, v_ref, o_ref, *, tq, tk, seq_len):
    g = pl.program_id(1)
    qi = pl.program_id(2)
    q0 = qi * tq
    start = jnp.clip(q0 - WINDOW, 0, seq_len - tk)
    start = pl.multiple_of(start, WINDOW)
    k = k_ref[0, pl.ds(start, tk), :]
    v = v_ref[0, pl.ds(start, tk), :]
    qpos = q0 + lax.broadcasted_iota(jnp.int32, (tq, tk), 0)
    kpos = start + lax.broadcasted_iota(jnp.int32, (tq, tk), 1)
    valid = jnp.abs(qpos - kpos) <= WINDOW
    group = A_HEADS // A_KV_HEADS
    for h in range(group):
        sl = slice(h * HEAD_DIM, (h + 1) * HEAD_DIM)
        s = lax.dot_general(q_ref[0, :, sl], k, (((1,), (1,)), ((), ())), preferred_element_type=F32)
        s = jnp.where(valid, s, NEG_INF)
        sink = sink_ref[g * group + h]
        m = jnp.maximum(jnp.max(s, axis=-1, keepdims=True), sink)
        p = jnp.exp(s - m)
        denom = jnp.sum(p, axis=-1, keepdims=True) + jnp.exp(sink - m)
        o = jnp.dot(p.astype(BF16), v, preferred_element_type=F32)
        o_ref[0, :, sl] = (o / denom).astype(o_ref.dtype)


def _attn_a(qkv, sink, tq=256):
    B, S, _ = qkv.shape
    tk = tq + 2 * WINDOW
    gw = (A_HEADS // A_KV_HEADS) * HEAD_DIM
    return pl.pallas_call(
        functools.partial(_attn_a_kernel, tq=tq, tk=tk, seq_len=S),
        grid_spec=pltpu.PrefetchScalarGridSpec(
            num_scalar_prefetch=1,
            grid=(B, A_KV_HEADS, S // tq),
            in_specs=[
                pl.BlockSpec((1, tq, gw), lambda b, g, i, s: (b, i, QA_H * HEAD_DIM // gw + g)),
                pl.BlockSpec((1, S, HEAD_DIM), lambda b, g, i, s: (b, 0, KA_H + g)),
                pl.BlockSpec((1, S, HEAD_DIM), lambda b, g, i, s: (b, 0, VA_H + g)),
            ],
            out_specs=pl.BlockSpec((1, tq, gw), lambda b, g, i, s: (b, i, g)),
        ),
        out_shape=jax.ShapeDtypeStruct((B, S, A_Q), BF16),
        compiler_params=_cparams("parallel", "parallel", "arbitrary"),
        name="attn_window",
    )(sink, qkv, qkv, qkv)


def _attn_b_kernel(q_ref, k_ref, vt_ref, o_ref, m_ref, l_ref, acc_ref):
    ki = pl.program_id(3)

    @pl.when(ki == 0)
    def _():
        m_ref[...] = jnp.full(m_ref.shape, -jnp.inf, F32)
        l_ref[...] = jnp.zeros(l_ref.shape, F32)
        acc_ref[...] = jnp.zeros(acc_ref.shape, F32)

    k = k_ref[0]
    vt = vt_ref[0]
    group = B_HEADS // B_KV_HEADS
    for h in range(group):
        sl = slice(h * HEAD_DIM, (h + 1) * HEAD_DIM)
        st = lax.dot_general(k, q_ref[0, :, sl], (((1,), (1,)), ((), ())), preferred_element_type=F32)
        m_prev = m_ref[h]
        m_cur = jnp.maximum(m_prev, jnp.max(st, axis=0, keepdims=True))
        alpha = jnp.exp2(m_prev - m_cur)
        pt = jnp.exp2(st - m_cur)
        l_ref[h] = alpha * l_ref[h] + jnp.sum(pt, axis=0, keepdims=True)
        acc_ref[h] = alpha * acc_ref[h] + jnp.dot(vt, pt.astype(BF16), preferred_element_type=F32)
        m_ref[h] = m_cur

    @pl.when(ki == pl.num_programs(3) - 1)
    def _():
        for h in range(group):
            sl = slice(h * HEAD_DIM, (h + 1) * HEAD_DIM)
            o_ref[0, :, sl] = (acc_ref[h] / l_ref[h]).T.astype(o_ref.dtype)


def _attn_b(qkv, tq=512, tk=8192):
    B, S, _ = qkv.shape
    tq, tk = min(tq, S), min(tk, S)
    group = B_HEADS // B_KV_HEADS
    gw = group * HEAD_DIM
    vt = jnp.swapaxes(qkv[:, :, VB_H * HEAD_DIM:(VB_H + B_KV_HEADS) * HEAD_DIM], 1, 2)
    return pl.pallas_call(
        _attn_b_kernel,
        grid=(B, B_KV_HEADS, S // tq, S // tk),
        in_specs=[
            pl.BlockSpec((1, tq, gw), lambda b, g, i, k: (b, i, QB_H * HEAD_DIM // gw + g)),
            pl.BlockSpec((1, tk, HEAD_DIM), lambda b, g, i, k: (b, k, KB_H + g)),
            pl.BlockSpec((1, HEAD_DIM, tk), lambda b, g, i, k: (b, g, k)),
        ],
        out_specs=pl.BlockSpec((1, tq, gw), lambda b, g, i, k: (b, i, g)),
        out_shape=jax.ShapeDtypeStruct((B, S, B_Q), BF16),
        scratch_shapes=[
            pltpu.VMEM((group, 1, tq), F32),
            pltpu.VMEM((group, 1, tq), F32),
            pltpu.VMEM((group, HEAD_DIM, tq), F32),
        ],
        compiler_params=_cparams("parallel", "parallel", "parallel", "arbitrary"),
        name="attn_dense",
    )(qkv, qkv, vt)


C_QROWS = 8
C_KROWS = 16
C_TQ = C_QROWS * GRID_W
C_TK = C_KROWS * GRID_W
C_HEADS_PER_STEP = 2


def _attn_c_kernel(q_ref, k_ref, vt_ref, bias_ref, o_ref, *, rows):
    i = pl.program_id(2)
    start_row = jnp.clip(i * C_QROWS - NB_ROWS // 2, 0, rows - C_KROWS)
    start = pl.multiple_of(start_row * GRID_W, (NB_ROWS // 2) * GRID_W)
    for h in range(C_HEADS_PER_STEP):
        sl = slice(h * HEAD_DIM, (h + 1) * HEAD_DIM)
        k = k_ref[0, pl.ds(start, C_TK), sl]
        vt = vt_ref[0, sl, pl.ds(start, C_TK)]
        st = lax.dot_general(k, q_ref[0, :, sl], (((1,), (1,)), ((), ())), preferred_element_type=F32)
        st = st + bias_ref[0, h]
        pt = jnp.exp(st - jnp.max(st, axis=0, keepdims=True))
        denom = jnp.sum(pt, axis=0, keepdims=True)
        ot = jnp.dot(vt, pt.astype(BF16), preferred_element_type=F32)
        o_ref[0, :, sl] = (ot / denom).T.astype(o_ref.dtype)


def _nbr_bias_tables(rpb, rows):
    assert rows % C_QROWS == 0 and rows >= C_KROWS
    rpb = rpb.astype(F32)
    c = jnp.arange(GRID_W)[:, None]
    kc = jnp.arange(GRID_W)[None, :]
    cs = jnp.clip(c - NB_COLS // 2, 0, GRID_W - NB_COLS)
    col_valid = (kc >= cs) & (kc < cs + NB_COLS)
    col_hot = (kc - c + (NB_COLS - 1))[..., None] == jnp.arange(2 * NB_COLS - 1)
    col_hot = (col_hot & col_valid[..., None]).astype(F32)
    by_col = jnp.einsum("hrc,qkc->hrqk", rpb, col_hot, precision=lax.Precision.HIGHEST)
    tables = []
    for first_row in (0, min(C_QROWS, rows - C_KROWS + NB_ROWS // 2), rows - C_QROWS):
        start = min(max(first_row - NB_ROWS // 2, 0), rows - C_KROWS)
        r = first_row + jnp.arange(C_QROWS)[:, None]
        kr = start + jnp.arange(C_KROWS)[None, :]
        rs = jnp.clip(r - NB_ROWS // 2, 0, rows - NB_ROWS)
        row_valid = (kr >= rs) & (kr < rs + NB_ROWS)
        row_hot = (kr - r + (NB_ROWS - 1))[..., None] == jnp.arange(2 * NB_ROWS - 1)
        row_hot = (row_hot & row_valid[..., None]).astype(F32)
        bias = jnp.einsum("djr,hrqk->hjkdq", row_hot, by_col, precision=lax.Precision.HIGHEST)
        valid = row_valid.T[:, None, :, None] & col_valid.T[None, :, None, :]
        bias = jnp.where(valid[None], bias, NEG_INF)
        tables.append(bias.reshape(C_HEADS, C_TK, C_TQ))
    return jnp.stack(tables)


def _attn_c(qkv, bias_tables):
    B, S, _ = qkv.shape
    rows = S // GRID_W
    n = S // C_TQ
    vt = jnp.swapaxes(qkv[:, :, VC_H * HEAD_DIM:(VC_H + C_HEADS) * HEAD_DIM], 1, 2)

    hps = C_HEADS_PER_STEP
    gw = hps * HEAD_DIM

    def bias_idx(h, b, i):
        return (jnp.where(i == 0, 0, jnp.where(i == n - 1, 2, 1)), h, 0, 0)

    return pl.pallas_call(
        functools.partial(_attn_c_kernel, rows=rows),
        grid=(C_HEADS // hps, B, n),
        in_specs=[
            pl.BlockSpec((1, C_TQ, gw), lambda h, b, i: (b, i, QC_H // hps + h)),
            pl.BlockSpec((1, S, gw), lambda h, b, i: (b, 0, KC_H // hps + h)),
            pl.BlockSpec((1, gw, S), lambda h, b, i: (b, h, 0)),
            pl.BlockSpec((1, hps, C_TK, C_TQ), bias_idx),
        ],
        out_specs=pl.BlockSpec((1, C_TQ, gw), lambda h, b, i: (b, i, h)),
        out_shape=jax.ShapeDtypeStruct((B, S, C_W), BF16),
        compiler_params=_cparams("parallel", "parallel", "arbitrary"),
        name="attn_nbr",
    )(qkv, qkv, vt, bias_tables)


def _merge_kernel(x_ref, g_ref, oa_ref, ob_ref, oc_ref, wga_ref, wgb_ref, wgc_ref, wa_ref, wb_ref, wc_ref,
                  o_ref, h_ref):
    @pl.when(pl.program_id(1) == 0)
    def _():
        h_ref[...] = _rms_rows(x_ref[...], g_ref[...]).astype(BF16)

    h = h_ref[...]
    merged = None
    for o_r, wg_r, w_r in ((oa_ref, wga_ref, wa_ref), (ob_ref, wgb_ref, wb_ref), (oc_ref, wgc_ref, wc_ref)):
        gate = jnp.dot(h, wg_r[0], preferred_element_type=F32)
        y = jnp.dot(o_r[...], w_r[0], preferred_element_type=F32)
        term = jax.nn.sigmoid(gate) * y
        merged = term if merged is None else merged + term
    o_ref[...] = merged.astype(o_ref.dtype)


def _merge(x2d, norm_g, oa, ob, oc, w_gates, wba, wbb, wbc, tm):
    T, D = x2d.shape
    tn = w_gates.shape[2]
    nj = D // tn
    o_spec = lambda w: pl.BlockSpec((tm, w), lambda i, j: (i, 0))
    wg_spec = lambda k: pl.BlockSpec((1, D, tn), lambda i, j: (k * nj + j, 0, 0))
    wb_spec = lambda w: pl.BlockSpec((1, w, tn), lambda i, j: (j, 0, 0))
    return pl.pallas_call(
        _merge_kernel,
        grid=(T // tm, nj),
        in_specs=[
            pl.BlockSpec((tm, D), lambda i, j: (i, 0)),
            pl.BlockSpec((1, D), lambda i, j: (0, 0)),
            o_spec(A_Q), o_spec(B_Q), o_spec(C_W),
            wg_spec(0), wg_spec(1), wg_spec(2),
            wb_spec(A_Q), wb_spec(B_Q), wb_spec(C_W),
        ],
        out_specs=pl.BlockSpec((tm, tn), lambda i, j: (i, j)),
        out_shape=jax.ShapeDtypeStruct((T, D), BF16),
        scratch_shapes=[pltpu.VMEM((tm, D), BF16)],
        compiler_params=_cparams("parallel", "arbitrary"),
        name="gated_merge",
    )(x2d, norm_g, oa, ob, oc, w_gates, w_gates, w_gates, wba, wbb, wbc)


def _outproj_kernel(x_ref, m_ref, w_ref, g_ref, wr_ref, x1_ref, h2_ref, lg_ref):
    x1 = x_ref[...] + jnp.dot(m_ref[...], w_ref[...], preferred_element_type=F32)
    x1_ref[...] = x1
    h2 = _rms_rows(x1, g_ref[...])
    hi = h2.astype(BF16)
    lo = (h2 - hi.astype(F32)).astype(BF16)
    h2_ref[...] = hi
    wr = wr_ref[...]
    r = jnp.dot(hi, wr, preferred_element_type=F32) + jnp.dot(lo, wr, preferred_element_type=F32)
    lg_ref[...] = r + pltpu.roll(r, ROUTER_LANES - N_EXPERTS, 1)


def _outproj(x2d, merged, w_out, norm_g, wr_cat, tm):
    T, D = x2d.shape
    return pl.pallas_call(
        _outproj_kernel,
        grid=(T // tm,),
        in_specs=[
            pl.BlockSpec((tm, D), lambda i: (i, 0)),
            pl.BlockSpec((tm, D), lambda i: (i, 0)),
            pl.BlockSpec((D, D), lambda i: (0, 0)),
            pl.BlockSpec((1, D), lambda i: (0, 0)),
            pl.BlockSpec((D, ROUTER_LANES), lambda i: (0, 0)),
        ],
        out_specs=[
            pl.BlockSpec((tm, D), lambda i: (i, 0)),
            pl.BlockSpec((tm, D), lambda i: (i, 0)),
            pl.BlockSpec((tm, ROUTER_LANES), lambda i: (i, 0)),
        ],
        out_shape=[
            jax.ShapeDtypeStruct((T, D), F32),
            jax.ShapeDtypeStruct((T, D), BF16),
            jax.ShapeDtypeStruct((T, ROUTER_LANES), F32),
        ],
        compiler_params=_cparams("parallel"),
        name="out_proj",
    )(x2d, merged, w_out, norm_g, wr_cat)


def _router_split(w_router):
    hi = w_router.astype(BF16)
    lo = (w_router - hi.astype(F32)).astype(BF16)
    pad = jnp.zeros((w_router.shape[0], ROUTER_LANES - 2 * N_EXPERTS), BF16)
    return jnp.concatenate([hi, lo, pad], axis=1)


def _expert_kernel(xs_ref, wg_ref, wu_ref, wd_ref, gs_ref, o_ref):
    f = pl.program_id(2)

    @pl.when(f == 0)
    def _():
        o_ref[0] = jnp.zeros(o_ref.shape[1:], F32)

    xs = xs_ref[0]
    a = jnp.dot(xs, wg_ref[0], preferred_element_type=F32)
    u = jnp.dot(xs, wu_ref[0], preferred_element_type=F32)
    hmid = (jax.nn.silu(a) * u).astype(BF16)
    o_ref[0] += jnp.dot(hmid, wd_ref[0], preferred_element_type=F32)

    @pl.when(f == pl.num_programs(2) - 1)
    def _():
        o_ref[0] *= gs_ref[0]


def _experts(xs, w_gate, w_up, w_down, gscale, layer, tm=1024, tf=512):
    E, C, D = xs.shape
    F = w_gate.shape[2]
    tm, tf = min(tm, C), min(tf, F)
    first = layer * E
    return pl.pallas_call(
        _expert_kernel,
        grid=(E, C // tm, F // tf),
        in_specs=[
            pl.BlockSpec((1, tm, D), lambda e, m, f: (e, m, 0)),
            pl.BlockSpec((1, D, tf), lambda e, m, f: (first + e, 0, f)),
            pl.BlockSpec((1, D, tf), lambda e, m, f: (first + e, 0, f)),
            pl.BlockSpec((1, tf, D), lambda e, m, f: (first + e, f, 0)),
            pl.BlockSpec((1, tm, 1), lambda e, m, f: (e, m, 0)),
        ],
        out_specs=pl.BlockSpec((1, tm, D), lambda e, m, f: (e, m, 0)),
        out_shape=jax.ShapeDtypeStruct((E, C, D), F32),
        compiler_params=_cparams("parallel", "parallel", "arbitrary"),
        name="expert_swiglu",
    )(xs, w_gate, w_up, w_down, gscale)


COMBINE_TM = 256
COMBINE_W = 256


def _combine_kernel(tile_ref, win_ref, first_ref, x1_ref, tok_ref, rows_ref, o_ref):
    g = pl.program_id(0)
    tm, w = o_ref.shape[0], rows_ref.shape[0]

    @pl.when(first_ref[g] == 1)
    def _():
        o_ref[...] = x1_ref[...]

    token = tile_ref[g] * tm + lax.broadcasted_iota(jnp.int32, (tm, w), 0)
    onehot = jnp.where(token == tok_ref[0], 1.0, 0.0).astype(BF16)
    rows = rows_ref[...]
    hi = rows.astype(BF16)
    lo = (rows - hi.astype(F32)).astype(BF16)
    o_ref[...] += (jnp.dot(onehot, hi, preferred_element_type=F32)
                   + jnp.dot(onehot, lo, preferred_element_type=F32))


def _combine_plan(idx, num_tokens, tm, w):
    n = idx.size
    n_tiles = num_tokens // tm
    n_win = n // w
    n_steps = n_win + n_tiles
    flat_tok = idx.reshape(-1).astype(jnp.int32)
    tok_sorted, order = lax.sort_key_val(flat_tok, jnp.arange(n, dtype=jnp.int32))
    bounds = jnp.arange(n_tiles + 1, dtype=jnp.int32) * tm
    starts = jnp.sum((tok_sorted[None, :] < bounds[:, None]).astype(jnp.int32), axis=1)
    first_win = jnp.minimum(starts[:-1] // w, n_win - 1)
    last_win = jnp.clip((starts[1:] - 1) // w, first_win, n_win - 1)
    step_start = jnp.concatenate([jnp.zeros((1,), jnp.int32), jnp.cumsum(last_win - first_win + 1)])
    g = jnp.arange(n_steps, dtype=jnp.int32)
    tile = jnp.minimum(jnp.sum((step_start[None, 1:] <= g[:, None]).astype(jnp.int32), axis=1), n_tiles - 1)
    of_tile = tile[:, None] == jnp.arange(n_tiles, dtype=jnp.int32)[None, :]
    tile_first_win = jnp.sum(jnp.where(of_tile, first_win[None, :], 0), axis=1)
    tile_step_start = jnp.sum(jnp.where(of_tile, step_start[None, :-1], 0), axis=1)
    active = g < step_start[n_tiles]
    win = jnp.where(active, tile_first_win + g - tile_step_start, n_win)
    first = (active & (g == tile_step_start)).astype(jnp.int32)
    order = jnp.concatenate([order, jnp.zeros((w,), jnp.int32)])
    tok = jnp.concatenate([tok_sorted, jnp.full((w,), -1, jnp.int32)]).reshape(n_win + 1, 1, w)
    return order, tok, tile, win, first


def _combine(x1, ye_flat, idx):
    T, D = x1.shape
    tm, w = min(COMBINE_TM, T), COMBINE_W
    order, tok, tile, win, first = _combine_plan(idx, T, tm, w)
    rows = ye_flat[order]
    return pl.pallas_call(
        _combine_kernel,
        grid_spec=pltpu.PrefetchScalarGridSpec(
            num_scalar_prefetch=3,
            grid=(tile.shape[0],),
            in_specs=[
                pl.BlockSpec((tm, D), lambda g, tile, win, first: (tile[g], 0)),
                pl.BlockSpec((1, 1, w), lambda g, tile, win, first: (win[g], 0, 0)),
                pl.BlockSpec((w, D), lambda g, tile, win, first: (win[g], 0)),
            ],
            out_specs=pl.BlockSpec((tm, D), lambda g, tile, win, first: (tile[g], 0)),
        ),
        out_shape=jax.ShapeDtypeStruct((T, D), F32),
        compiler_params=_cparams("arbitrary"),
        name="expert_combine",
    )(tile, win, first, x1, tok, rows)


def _layer(x, lw, tabs, bias_tables):
    B, S, D = x.shape
    T = B * S
    tm = min(512, S)
    x2d = x.reshape(T, D)
    qkv = _qkv_proj(x2d, lw["norm_attn"], lw["w_qkv"], lw["qkn"], tabs, S, tm).reshape(B, S, QKV_W)
    oa = _attn_a(qkv, lw["sink_a"]).reshape(T, A_Q)
    ob = _attn_b(qkv).reshape(T, B_Q)
    oc = _attn_c(qkv, bias_tables).reshape(T, C_W)
    merged = _merge(x2d, lw["norm_attn"], oa, ob, oc, lw["w_gates"], lw["wba"], lw["wbb"], lw["wbc"], tm)
    x1, h2, logits = _outproj(x2d, merged, lw["w_out"], lw["norm_ffn"], lw["wr_cat"], min(256, S))

    cap = CAPACITY_FACTOR * T // N_EXPERTS
    aff = jax.nn.softmax(logits[:, :N_EXPERTS], axis=-1)
    gsel, idx = lax.top_k(aff.T, cap)
    xs = h2[idx]
    ye = _experts(xs, lw["w_gate"], lw["w_up"], lw["w_down"], gsel[..., None], lw["layer"])
    return _combine(x1, ye.reshape(-1, D), idx).reshape(B, S, D)


def kernel(x_prompt, x_sample, norm_attn, w_in, qk_norm, sink_a, rpb_c, w_branch_a, w_branch_b, w_branch_c,
           w_out, norm_ffn, w_router, w_gate, w_up, w_down):
    depth = w_in.shape[0]
    qkv_blocks = [b for o, n in W_IN_SEGMENTS for b in range(o // PREP_COLS, (o + n) // PREP_COLS)]
    gate_blocks = list(range(QKV_W // PREP_COLS, w_in.shape[2] // PREP_COLS))
    layers = []
    for l in range(depth):
        layers.append(dict(
            norm_attn=norm_attn[l][None, :],
            w_qkv=_prep_col_tiles(w_in, l, qkv_blocks, QKV_TN),
            w_gates=_prep_col_tiles(w_in, l, gate_blocks, MATMUL_TN),
            qkn=jnp.pad(qk_norm[l], ((0, 2), (0, 0))),
            sink_a=sink_a[l],
            rpb=rpb_c[l],
            wba=_col_tiles(w_branch_a[l].astype(BF16), MATMUL_TN),
            wbb=_col_tiles(w_branch_b[l].astype(BF16), MATMUL_TN),
            wbc=_col_tiles(w_branch_c[l].astype(BF16), MATMUL_TN),
            w_out=w_out[l].astype(BF16),
            norm_ffn=norm_ffn[l][None, :],
            wr_cat=_router_split(w_router[l]),
            layer=l,
        ))
    stack = lambda w: w.astype(BF16).reshape((-1,) + w.shape[2:])
    ffn = dict(w_gate=stack(w_gate), w_up=stack(w_up), w_down=stack(w_down))
    for lw in layers:
        lw.update(ffn)

    def trunk(x):
        S = x.shape[1]
        tabs = _rope_tables(S)
        for lw in layers:
            x = _layer(x, lw, tabs, _nbr_bias_tables(lw["rpb"], S // GRID_W))
        return x

    return trunk(x_prompt), trunk(x_sample)
```
